```python
import jax, jax.numpy as jnp
from jax import lax
import numpy as np

D_MODEL = 4096
BATCH = 4
SEQ = 4096
DEPTH = 1

N_META = 16
DSA_HEADS = 16
DSA_HEAD_DIM = 128
DSA_Q_RANK = 768
DSA_KV_RANK = 512
IDX_HEADS = 16
IDX_DIM = 64
IDX_TOPK_MAX = 256
Q_BLOCK = 128
DSA_SCALE = DSA_HEAD_DIM ** -0.5
HG_HEADS = 16
HG_DK = 128
HG_DV = 128
HG_WIDTH = HG_HEADS * HG_DK
HG_CHUNK = 64
N_EXPERTS = 32
TOP_K = 4
D_EXPERT = 1536
SWIGLU_LIMIT = 7.0
SWIGLU_ALPHA = 1.702
MOE_BLOCK = 256
LN_EPS = 1e-5
RMS_EPS = 1e-6
DN_ALPHA = (2 * DEPTH) ** 0.25
DN_BETA = (8 * DEPTH) ** -0.25
MIX_WIDTH = DSA_HEADS * DSA_HEAD_DIM + HG_HEADS * HG_DV
IN_WIDTHS = (DSA_Q_RANK, DSA_KV_RANK, IDX_DIM, IDX_HEADS, HG_WIDTH, HG_WIDTH, HG_HEADS * HG_DV, HG_HEADS * HG_DV)
IN_WIDTH = sum(IN_WIDTHS)
SPLIT_POINTS = tuple(int(v) for v in np.cumsum(IN_WIDTHS)[:-1])

kernel_name = 'hybrid_dsa_hgrn2_moe_block'


def layer_norm(x, g, b):
    xf = x.astype(jnp.float32)
    mu = xf.mean(-1, keepdims=True)
    var = jnp.square(xf - mu).mean(-1, keepdims=True)
    return ((xf - mu) * lax.rsqrt(var + LN_EPS) * g.astype(jnp.float32) + b.astype(jnp.float32)).astype(x.dtype)


def rms_norm(x, g):
    xf = x.astype(jnp.float32)
    ms = jnp.square(xf).mean(-1, keepdims=True)
    return (xf * lax.rsqrt(ms + RMS_EPS) * g.astype(jnp.float32)).astype(x.dtype)


def alibi_slopes(n_heads):
    return 2.0 ** (-8.0 * (jnp.arange(n_heads, dtype=jnp.float32) + 1.0) / n_heads)


def dsa_mixer(c_q, c_kv, ik, iw, q_norm_g, kv_norm_g, w_uq, w_uk, w_uv, w_iq, ik_g, ik_b):
    B, T, _ = c_q.shape
    L = T - N_META
    k_sel = min(IDX_TOPK_MAX, L // 4)
    dt = c_q.dtype
    cq = rms_norm(c_q, q_norm_g)
    ckv = rms_norm(c_kv, kv_norm_g)
    q = (cq @ w_uq).reshape(B, T, DSA_HEADS, DSA_HEAD_DIM)
    iq = (cq @ w_iq).reshape(B, T, IDX_HEADS, IDX_DIM)
    ikn = layer_norm(ik, ik_g, ik_b)
    iwt = iw.astype(jnp.float32) * (IDX_HEADS ** -0.5 * IDX_DIM ** -0.5)
    slopes = alibi_slopes(DSA_HEADS)
    ckv_m, ckv_r = ckv[:, :N_META], ckv[:, N_META:]
    ik_r = ikn[:, N_META:]

    qa_m = jnp.einsum('bqhd,rhd->bqhr', q[:, :N_META], w_uk)
    pos_m = jnp.arange(N_META)
    dist = (pos_m[:, None] - pos_m[None, :]).astype(jnp.float32)
    lm = jnp.einsum('bqhr,bmr->bqhm', qa_m, ckv_m).astype(jnp.float32) * DSA_SCALE
    lm = lm - slopes[None, None, :, None] * dist[None, :, None, :]
    lm = jnp.where((dist >= 0)[None, :, None, :], lm, -jnp.inf)
    pm = jax.nn.softmax(lm, axis=-1).astype(dt)
    out_meta = jnp.einsum('bqhr,rhd->bqhd', jnp.einsum('bqhm,bmr->bqhr', pm, ckv_m), w_uv)

    nb = L // Q_BLOCK

    def to_blocks(a):
        return a[:, N_META:].reshape((B, nb, Q_BLOCK) + a.shape[2:]).swapaxes(0, 1)

    def block(args):
        qb, iqb, iwb, start = args
        t = start + jnp.arange(Q_BLOCK)
        sc = jax.nn.relu(jnp.einsum('bqhd,bsd->bqhs', iqb, ik_r)).astype(jnp.float32)
        sc = jnp.einsum('bqhs,bqh->bqs', sc, iwb)
        vis = jnp.arange(L)[None, :] <= t[:, None]
        sc = jnp.where(vis[None], sc, -jnp.inf)
        _, sel = lax.top_k(sc, k_sel)
        valid = sel <= t[None, :, None]
        ckv_sel = jax.vmap(lambda c, i: c[i])(ckv_r, sel)
        qa = jnp.einsum('bqhd,rhd->bqhr', qb, w_uk)
        ls = jnp.einsum('bqhr,bqkr->bqhk', qa, ckv_sel).astype(jnp.float32) * DSA_SCALE
        ls = ls - slopes[None, None, :, None] * (t[None, :, None] - sel).astype(jnp.float32)[:, :, None, :]
        ls = jnp.where(valid[:, :, None, :], ls, -jnp.inf)
        lmb = jnp.einsum('bqhr,bmr->bqhm', qa, ckv_m).astype(jnp.float32) * DSA_SCALE
        dist_m = (t[:, None] + N_META - jnp.arange(N_META)[None, :]).astype(jnp.float32)
        lmb = lmb - slopes[None, None, :, None] * dist_m[None, :, None, :]
        p = jax.nn.softmax(jnp.concatenate([lmb, ls], axis=-1), axis=-1).astype(dt)
        lat = (jnp.einsum('bqhm,bmr->bqhr', p[..., :N_META], ckv_m)
               + jnp.einsum('bqhk,bqkr->bqhr', p[..., N_META:], ckv_sel))
        return jnp.einsum('bqhr,rhd->bqhd', lat, w_uv)

    starts = jnp.arange(nb, dtype=jnp.int32) * Q_BLOCK
    out_r = lax.map(block, (to_blocks(q), to_blocks(iq), to_blocks(iwt), starts))
    out_r = out_r.swapaxes(0, 1).reshape(B, L, DSA_HEADS, DSA_HEAD_DIM)
    return jnp.concatenate([out_meta, out_r], axis=1).reshape(B, T, DSA_HEADS * DSA_HEAD_DIM)


def _hgrn2_chunk(S, inp):
    q, k, v, lf = inp
    C = q.shape[2]
    b = jnp.cumsum(lf, axis=2)
    causal = jnp.tril(jnp.ones((C, C), dtype=bool))
    decay = jnp.exp(jnp.where(causal[:, :, None], b[:, :, :, None, :] - b[:, :, None, :, :], -jnp.inf))
    attn = jnp.einsum('bhtd,bhsd,bhtsd->bhts', q, k, decay)
    o = jnp.einsum('bhtd,bhdv->bhtv', q * jnp.exp(b), S) + jnp.einsum('bhts,bhsv->bhtv', attn, v)
    b_last = b[:, :, -1, :]
    S = jnp.exp(b_last)[..., None] * S + jnp.einsum('bhsd,bhsv->bhdv', k * jnp.exp(b_last[:, :, None, :] - b), v)
    return S, o


def hgrn2_mixer(hq, hf, hi, hg, lb, norm_g):
    B, T, _ = hq.shape
    f32 = jnp.float32
    q = jax.nn.silu(hq.astype(f32))
    lbf = lb.astype(f32)
    logf = jnp.logaddexp(jnp.log(lbf), jnp.log1p(-lbf) + jax.nn.log_sigmoid(hf.astype(f32)))
    k = -jnp.expm1(logf)
    v = hi.astype(f32)
    pad = HG_CHUNK - N_META
    Tp = T + pad
    nc = Tp // HG_CHUNK

    def chunks(a, d):
        a = jnp.pad(a, ((0, 0), (pad, 0), (0, 0)))
        return a.reshape(B, nc, HG_CHUNK, HG_HEADS, d).transpose(1, 0, 3, 2, 4)

    S0 = jnp.zeros((B, HG_HEADS, HG_DK, HG_DV), f32)
    _, o = lax.scan(_hgrn2_chunk, S0, (chunks(q, HG_DK), chunks(k, HG_DK), chunks(v, HG_DV), chunks(logf, HG_DK)))
    o = o.transpose(1, 0, 3, 2, 4).reshape(B, Tp, HG_HEADS, HG_DV)[:, pad:]
    o = rms_norm(o, norm_g.reshape(HG_HEADS, HG_DV))
    o = o * jax.nn.silu(hg.astype(f32)).reshape(B, T, HG_HEADS, HG_DV)
    return o.reshape(B, T, HG_HEADS * HG_DV).astype(hq.dtype)


def moe_ffn(h, w_router, b_router, w_gate, b_gate, w_up, b_up, w_down, b_down):
    B, T, D = h.shape
    N = B * T
    NK = N * TOP_K
    xt = h.reshape(N, D)
    logits = (xt @ w_router + b_router).astype(jnp.float32)
    top_val, top_idx = lax.top_k(logits, TOP_K)
    gate = jax.nn.softmax(top_val, axis=-1)
    flat_e = top_idx.reshape(NK).astype(jnp.int32)
    flat_tok = jnp.arange(NK, dtype=jnp.int32) // TOP_K
    order = jnp.argsort(flat_e)
    se, stok, sgate = flat_e[order], flat_tok[order], gate.reshape(NK)[order]
    counts = jnp.bincount(flat_e, length=N_EXPERTS).astype(jnp.int32)
    padded = (counts + MOE_BLOCK - 1) // MOE_BLOCK * MOE_BLOCK
    pad_end = jnp.cumsum(padded)
    pad_start = pad_end - padded
    start = jnp.cumsum(counts) - counts
    dest = pad_start[se] + jnp.arange(NK, dtype=jnp.int32) - start[se]
    n_blk = -(-(NK + N_EXPERTS * (MOE_BLOCK - 1)) // MOE_BLOCK)
    cap = n_blk * MOE_BLOCK
    slot_tok = jnp.full((cap,), N, jnp.int32).at[dest].set(stok)
    slot_gate = jnp.zeros((cap,), jnp.float32).at[dest].set(sgate)
    blk_expert = jnp.minimum(jnp.searchsorted(pad_end, jnp.arange(n_blk) * MOE_BLOCK, side='right'), N_EXPERTS - 1)
    x_pad = jnp.concatenate([xt, jnp.zeros((1, D), xt.dtype)], axis=0)

    def run_block(args):
        tok, g, e = args
        xb = x_pad[tok]
        a = jnp.minimum(xb @ w_gate[e] + b_gate[e], SWIGLU_LIMIT)
        u = jnp.clip(xb @ w_up[e] + b_up[e], -SWIGLU_LIMIT, SWIGLU_LIMIT)
        y = ((u + 1.0) * a * jax.nn.sigmoid(SWIGLU_ALPHA * a)) @ w_down[e] + b_down[e]
        return y * g[:, None].astype(y.dtype)

    y = lax.map(run_block, (slot_tok.reshape(n_blk, MOE_BLOCK), slot_gate.reshape(n_blk, MOE_BLOCK), blk_expert))
    out = jax.ops.segment_sum(y.reshape(cap, D), slot_tok, num_segments=N + 1)[:N]
    return out.reshape(B, T, D)


def setup_inputs(seed: int = 0) -> dict:
    key = jax.random.key(seed)
    ks = jax.random.split(key, 32)
    f32 = jnp.float32
    L = DEPTH

    def nrm(k, shape, scale):
        return jax.random.normal(k, shape, f32) * scale

    return {
        'x': nrm(ks[0], (BATCH, SEQ, D_MODEL), 1.0),
        'meta_tokens': nrm(ks[1], (N_META, D_MODEL), 1.0),
        'emb_ln_g': 1.0 + nrm(ks[2], (D_MODEL,), 0.02),
        'emb_ln_b': nrm(ks[3], (D_MODEL,), 0.02),
        'lb_logits': nrm(ks[4], (DEPTH + 1, HG_WIDTH), 0.5),
        'w_in': nrm(ks[5], (L, D_MODEL, IN_WIDTH), D_MODEL ** -0.5),
        'q_norm_g': 1.0 + nrm(ks[6], (L, DSA_Q_RANK), 0.02),
        'kv_norm_g': 1.0 + nrm(ks[7], (L, DSA_KV_RANK), 0.02),
        'w_uq': nrm(ks[8], (L, DSA_Q_RANK, DSA_HEADS * DSA_HEAD_DIM), DSA_Q_RANK ** -0.5),
        'w_uk': nrm(ks[9], (L, DSA_KV_RANK, DSA_HEADS, DSA_HEAD_DIM), DSA_KV_RANK ** -0.5),
        'w_uv': nrm(ks[10], (L, DSA_KV_RANK, DSA_HEADS, DSA_HEAD_DIM), DN_BETA * DSA_KV_RANK ** -0.5),
        'w_iq': nrm(ks[11], (L, DSA_Q_RANK, IDX_HEADS * IDX_DIM), DSA_Q_RANK ** -0.5),
        'idx_k_ln_g': 1.0 + nrm(ks[12], (L, IDX_DIM), 0.02),
        'idx_k_ln_b': nrm(ks[13], (L, IDX_DIM), 0.02),
        'hgrn_norm_g': 1.0 + nrm(ks[14], (L, HG_HEADS * HG_DV), 0.02),
        'w_o': nrm(ks[15], (L, MIX_WIDTH, D_MODEL), DN_BETA * MIX_WIDTH ** -0.5),
        'ln1_g': 1.0 + nrm(ks[16], (L, D_MODEL), 0.02),
        'ln1_b': nrm(ks[17], (L, D_MODEL), 0.02),
        'w_router': nrm(ks[18], (L, D_MODEL, N_EXPERTS), D_MODEL ** -0.5),
        'b_router': nrm(ks[19], (L, N_EXPERTS), 0.01),
        'w_gate': nrm(ks[20], (L, N_EXPERTS, D_MODEL, D_EXPERT), D_MODEL ** -0.5),
        'b_gate': nrm(ks[21], (L, N_EXPERTS, D_EXPERT), 0.01),
        'w_up': nrm(ks[22], (L, N_EXPERTS, D_MODEL, D_EXPERT), D_MODEL ** -0.5),
        'b_up': nrm(ks[23], (L, N_EXPERTS, D_EXPERT), 0.01),
        'w_down': nrm(ks[24], (L, N_EXPERTS, D_EXPERT, D_MODEL), DN_BETA * D_EXPERT ** -0.5),
        'b_down': nrm(ks[25], (L, N_EXPERTS, D_MODEL), 0.01),
        'ln2_g': 1.0 + nrm(ks[26], (L, D_MODEL), 0.02),
        'ln2_b': nrm(ks[27], (L, D_MODEL), 0.02),
    }


def reference(x, meta_tokens, emb_ln_g, emb_ln_b, lb_logits, w_in, q_norm_g, kv_norm_g, w_uq, w_uk, w_uv, w_iq,
              idx_k_ln_g, idx_k_ln_b, hgrn_norm_g, w_o, ln1_g, ln1_b, w_router, b_router, w_gate, b_gate,
              w_up, b_up, w_down, b_down, ln2_g, ln2_b):
    B = x.shape[0]
    meta = jnp.broadcast_to(meta_tokens[None].astype(x.dtype), (B, N_META, D_MODEL))
    h = layer_norm(jnp.concatenate([meta, x], axis=1), emb_ln_g, emb_ln_b)
    lb_all = jnp.cumsum(jax.nn.softmax(lb_logits.astype(jnp.float32), axis=0), axis=0)
    for l in range(DEPTH):
        proj = h @ w_in[l]
        c_q, c_kv, ik, iw, hq, hf, hi, hg = jnp.split(proj, SPLIT_POINTS, axis=-1)
        a = dsa_mixer(c_q, c_kv, ik, iw, q_norm_g[l], kv_norm_g[l], w_uq[l], w_uk[l], w_uv[l], w_iq[l],
                      idx_k_ln_g[l], idx_k_ln_b[l])
        g = hgrn2_mixer(hq, hf, hi, hg, lb_all[l], hgrn_norm_g[l])
        mix = jnp.concatenate([a.astype(h.dtype), g.astype(h.dtype)], axis=-1) @ w_o[l]
        h = layer_norm(DN_ALPHA * h + mix, ln1_g[l], ln1_b[l])
        ffn = moe_ffn(h, w_router[l], b_router[l], w_gate[l], b_gate[l], w_up[l], b_up[l], w_down[l], b_down[l])
        h = layer_norm(DN_ALPHA * h + ffn.astype(h.dtype), ln2_g[l], ln2_b[l])
    return h[:, N_META:]
```

```python
import functools
import math

import jax
import jax.numpy as jnp
from jax import lax
from jax.experimental import pallas as pl
from jax.experimental.pallas import tpu as pltpu

F32 = jnp.float32
BF16 = jnp.bfloat16
I32 = jnp.int32

N_META = 16
DSA_HEADS = 16
DSA_HEAD_DIM = 128
DSA_Q_RANK = 768
DSA_KV_RANK = 512
IDX_HEADS = 16
IDX_DIM = 64
IDX_TOPK_MAX = 256
HG_HEADS = 16
HG_DK = 128
HG_DV = 128
HG_WIDTH = HG_HEADS * HG_DK
HG_CHUNK = 64
HG_SUB = 16
N_EXPERTS = 32
TOP_K = 4
SWIGLU_LIMIT = 7.0
SWIGLU_ALPHA = 1.702
LN_EPS = 1e-5
RMS_EPS = 1e-6
DEPTH = 1
DN_ALPHA = (2 * DEPTH) ** 0.25

V7X_VMEM_LIMIT_BYTES = 56 * 1024 * 1024
MOE_ROWS = 512
INT_MIN = -(2 ** 31)
NEG_INF = float("-inf")


def _params(sem):
    return pltpu.CompilerParams(dimension_semantics=sem, vmem_limit_bytes=V7X_VMEM_LIMIT_BYTES)


def _nt_dot(a, b):
    return lax.dot_general(a, b, (((1,), (1,)), ((), ())), preferred_element_type=F32)


def _dot(a, b):
    return jnp.dot(a, b, preferred_element_type=F32)


def _layer_norm(x, g, b):
    mu = jnp.mean(x, axis=-1, keepdims=True)
    xc = x - mu
    var = jnp.mean(xc * xc, axis=-1, keepdims=True)
    return xc * lax.rsqrt(var + LN_EPS) * g + b


def _rms_norm(x, g):
    ms = jnp.mean(x * x, axis=-1, keepdims=True)
    return x * lax.rsqrt(ms + RMS_EPS) * g


def _log_sigmoid(x):
    return jnp.minimum(x, 0.0) - jnp.log1p(jnp.exp(-jnp.abs(x)))


def _sigmoid(x):
    return 1.0 / (1.0 + jnp.exp(-x))


def _ln_body(x_ref, g_ref, b_ref, o32_ref, o16_ref):
    y = _layer_norm(x_ref[...].astype(F32), g_ref[...], b_ref[...])
    o32_ref[...] = y
    o16_ref[...] = y.astype(BF16)


def _ln_rows(x, g, b, tm):
    m, d = x.shape
    tm = min(tm, m)
    row = pl.BlockSpec((tm, d), lambda i: (i, 0))
    vec = pl.BlockSpec((1, d), lambda i: (0, 0))
    return pl.pallas_call(
        _ln_body,
        grid=(pl.cdiv(m, tm),),
        in_specs=[row, vec, vec],
        out_specs=[row, row],
        out_shape=[jax.ShapeDtypeStruct((m, d), F32), jax.ShapeDtypeStruct((m, d), BF16)],
        compiler_params=_params(("parallel",)),
        name="embed_ln",
    )(x, g.reshape(1, d), b.reshape(1, d))


def _mm_body(a_ref, b_ref, o_ref):
    o_ref[...] = _dot(a_ref[...], b_ref[...]).astype(o_ref.dtype)


def _matmul(a, b, out_dtype, tm, tn, name):
    m, k = a.shape
    n = b.shape[1]
    tm = min(tm, m)
    tn = min(tn, n)
    return pl.pallas_call(
        _mm_body,
        grid=(pl.cdiv(m, tm), pl.cdiv(n, tn)),
        in_specs=[pl.BlockSpec((tm, k), lambda i, j: (i, 0)),
                  pl.BlockSpec((k, tn), lambda i, j: (0, j))],
        out_specs=pl.BlockSpec((tm, tn), lambda i, j: (i, j)),
        out_shape=jax.ShapeDtypeStruct((m, n), out_dtype),
        compiler_params=_params(("parallel", "arbitrary")),
        name=name,
    )(a, b)


def _dsa_prep_body(x_ref, qg_ref, kvg_ref, ikg_ref, ikb_ref, wuq_ref, wukt_ref, wiq_ref,
                   qa_ref, iq_ref, ckv_ref, ikn_ref, iwt_ref, cq_sc, *, iw_scale):
    h = pl.program_id(1)
    rq = qg_ref.shape[1]
    rkv = kvg_ref.shape[1]
    di = ikg_ref.shape[1]

    @pl.when(h == 0)
    def _():
        x = x_ref[...]
        cq_sc[...] = _rms_norm(x[:, :rq], qg_ref[...]).astype(BF16)
        ckv_ref[...] = _rms_norm(x[:, rq:rq + rkv], kvg_ref[...]).astype(BF16)
        ik = x[:, rq + rkv:rq + rkv + di]
        ikn_ref[...] = _layer_norm(ik, ikg_ref[...], ikb_ref[...]).astype(BF16)
        iwt_ref[...] = x[:, rq + rkv + di:] * iw_scale

    cq = cq_sc[...]
    q = _dot(cq, wuq_ref[0]).astype(BF16)
    qa_ref[0] = _dot(q, wukt_ref[0]).astype(BF16)
    iq_ref[0] = _dot(cq, wiq_ref[0]).astype(BF16)


def _dsa_prep(x, q_g, kv_g, ik_g, ik_b, w_uq3, w_ukt3, w_iq3, tm):
    m, w = x.shape
    nh, rq, dh = w_uq3.shape
    rkv = w_ukt3.shape[2]
    di = w_iq3.shape[2]
    nih = w - rq - rkv - di
    assert nih == nh
    tm = min(tm, m)
    vec = lambda n: pl.BlockSpec((1, n), lambda i, h: (0, 0))
    return pl.pallas_call(
        functools.partial(_dsa_prep_body, iw_scale=nih ** -0.5 * di ** -0.5),
        grid=(pl.cdiv(m, tm), nh),
        in_specs=[pl.BlockSpec((tm, w), lambda i, h: (i, 0)),
                  vec(rq), vec(rkv), vec(di), vec(di),
                  pl.BlockSpec((1, rq, dh), lambda i, h: (h, 0, 0)),
                  pl.BlockSpec((1, dh, rkv), lambda i, h: (h, 0, 0)),
                  pl.BlockSpec((1, rq, di), lambda i, h: (h, 0, 0))],
        out_specs=[pl.BlockSpec((1, tm, rkv), lambda i, h: (h, i, 0)),
                   pl.BlockSpec((1, tm, di), lambda i, h: (h, i, 0)),
                   pl.BlockSpec((tm, rkv), lambda i, h: (i, 0)),
                   pl.BlockSpec((tm, di), lambda i, h: (i, 0)),
                   pl.BlockSpec((tm, nih), lambda i, h: (i, 0))],
        out_shape=[jax.ShapeDtypeStruct((nh, m, rkv), BF16),
                   jax.ShapeDtypeStruct((nh, m, di), BF16),
                   jax.ShapeDtypeStruct((m, rkv), BF16),
                   jax.ShapeDtypeStruct((m, di), BF16),
                   jax.ShapeDtypeStruct((m, nih), F32)],
        scratch_shapes=[pltpu.VMEM((tm, rq), BF16)],
        compiler_params=_params(("parallel", "arbitrary")),
        name="dsa_prep",
    )(x, q_g.reshape(1, rq), kv_g.reshape(1, rkv), ik_g.reshape(1, di), ik_b.reshape(1, di),
      w_uq3, w_ukt3, w_iq3)


def _dsa_attn_body(qa_ref, iq_ref, iwt_ref, ik_ref, ckv_ref, ckvm_ref, wuv_ref, o_ref,
                   key_sc, m_sc, l_sc, acc_sc, p_sc, *, k_sel, n_meta, scale, slopes):
    nh, tq, rkv = qa_ref.shape
    di = iq_ref.shape[2]
    seq = ik_ref.shape[0]
    nkb_all, _, kblk = key_sc.shape
    mpad = ckvm_ref.shape[0]
    dh = wuv_ref.shape[2]
    rows = nh * tq
    qi = pl.program_id(1)
    t0 = qi * tq
    t_col = t0 + lax.broadcasted_iota(I32, (tq, 1), 0)
    hs = [slice(h * tq, (h + 1) * tq) for h in range(nh)]

    iq2 = iq_ref[...].reshape(rows, di)
    iw = iwt_ref[...]
    for cb in range(nkb_all):
        @pl.when(cb * kblk < t0 + tq)
        def _():
            sc = _nt_dot(iq2, ik_ref[cb * kblk:(cb + 1) * kblk, :])
            tot = jnp.zeros((tq, kblk), F32)
            for h in range(nh):
                tot = tot + iw[:, h:h + 1] * jnp.maximum(sc[hs[h]], 0.0)
            bits = lax.bitcast_convert_type(tot, I32)
            key = jnp.where(bits < 0, bits ^ jnp.int32(0x7FFFFFFF), bits)
            col = cb * kblk + lax.broadcasted_iota(I32, (1, kblk), 1)
            key_sc[cb] = jnp.where(col <= t_col, key, jnp.int32(INT_MIN))

        @pl.when(cb * kblk >= t0 + tq)
        def _():
            key_sc[cb] = jnp.full((tq, kblk), INT_MIN, I32)

    def bisect(i, thr):
        cand = thr + lax.shift_left(jnp.int32(1), 31 - i)
        cnt = jnp.zeros((tq, 1), F32)
        for cb in range(nkb_all):
            cnt = cnt + jnp.sum((key_sc[cb] >= cand).astype(F32), axis=1, keepdims=True)
        return jnp.where(cnt >= float(k_sel), cand, thr)

    thr = lax.fori_loop(0, 32, bisect, jnp.full((tq, 1), INT_MIN, I32))
    thr = jnp.maximum(thr, jnp.int32(INT_MIN + 1))

    qa2 = qa_ref[...].reshape(rows, rkv)
    ckvm = ckvm_ref[...]
    sm = _nt_dot(qa2, ckvm) * scale
    jm = lax.broadcasted_iota(I32, (1, mpad), 1)
    dist_m = (t_col + n_meta - jm).astype(F32)
    for h in range(nh):
        s_h = jnp.where(jm < n_meta, sm[hs[h]] - slopes[h] * dist_m, NEG_INF)
        m_h = jnp.max(s_h, axis=1, keepdims=True)
        p = jnp.exp(s_h - m_h)
        m_sc[hs[h]] = m_h
        l_sc[hs[h]] = jnp.sum(p, axis=1, keepdims=True)
        p_sc[hs[h], :mpad] = p.astype(BF16)
    acc_sc[...] = _dot(p_sc[:, :mpad], ckvm)

    def kv_step(kb, carry):
        off = pl.multiple_of(kb * kblk, kblk)
        kv = ckv_ref[pl.ds(off, kblk), :]
        s = _nt_dot(qa2, kv) * scale
        col = off + lax.broadcasted_iota(I32, (1, kblk), 1)
        dist = (t_col - col).astype(F32)
        neg = jnp.where(key_sc[kb] >= thr, 0.0, NEG_INF)
        for h in range(nh):
            s_h = s[hs[h]] - slopes[h] * dist + neg
            m_old = m_sc[hs[h]]
            m_new = jnp.maximum(m_old, jnp.max(s_h, axis=1, keepdims=True))
            alpha = jnp.exp(m_old - m_new)
            p = jnp.exp(s_h - m_new)
            l_sc[hs[h]] = alpha * l_sc[hs[h]] + jnp.sum(p, axis=1, keepdims=True)
            m_sc[hs[h]] = m_new
            acc_sc[hs[h]] = acc_sc[hs[h]] * alpha
            p_sc[hs[h]] = p.astype(BF16)
        acc_sc[...] += _dot(p_sc[...], kv)
        return carry

    lax.fori_loop(0, (t0 + tq - 1) // kblk + 1, kv_step, 0)

    for h in range(nh):
        lat = (acc_sc[hs[h]] / l_sc[hs[h]]).astype(BF16)
        o_ref[:, h * dh:(h + 1) * dh] = _dot(lat, wuv_ref[h]).astype(o_ref.dtype)


def _dsa_attention(qa, iq, iwt, ikn, ckv, ckv_m, w_uv3, batch, seq, k_sel, tq, kblk):
    nh, m, rkv = qa.shape
    di = iq.shape[2]
    dh = w_uv3.shape[2]
    n_meta = ckv_m.shape[0]
    mpad = 128
    assert n_meta <= mpad <= kblk and seq % kblk == 0 and seq % tq == 0 and m == batch * seq
    ckv_mp = jnp.zeros((mpad, rkv), BF16).at[:n_meta].set(ckv_m)
    nq = seq // tq
    slopes = tuple(2.0 ** (-8.0 * (h + 1) / nh) for h in range(nh))
    body = functools.partial(_dsa_attn_body, k_sel=k_sel, n_meta=n_meta,
                             scale=dh ** -0.5, slopes=slopes)
    return pl.pallas_call(
        body,
        grid=(batch, nq),
        in_specs=[pl.BlockSpec((nh, tq, rkv), lambda b, q: (0, b * nq + q, 0)),
                  pl.BlockSpec((nh, tq, di), lambda b, q: (0, b * nq + q, 0)),
                  pl.BlockSpec((tq, nh), lambda b, q: (b * nq + q, 0)),
                  pl.BlockSpec((seq, di), lambda b, q: (b, 0)),
                  pl.BlockSpec((seq, rkv), lambda b, q: (b, 0)),
                  pl.BlockSpec((mpad, rkv), lambda b, q: (0, 0)),
                  pl.BlockSpec((nh, rkv, dh), lambda b, q: (0, 0, 0))],
        out_specs=pl.BlockSpec((tq, nh * dh), lambda b, q: (b * nq + q, 0)),
        out_shape=jax.ShapeDtypeStruct((m, nh * dh), BF16),
        scratch_shapes=[pltpu.VMEM((seq // kblk, tq, kblk), I32),
                        pltpu.VMEM((nh * tq, 1), F32),
                        pltpu.VMEM((nh * tq, 1), F32),
                        pltpu.VMEM((nh * tq, rkv), F32),
                        pltpu.VMEM((nh * tq, kblk), BF16)],
        compiler_params=_params(("parallel", "arbitrary")),
        name="dsa_attention",
    )(qa, iq, iwt, ikn, ckv, ckv_mp, w_uv3)


def _hgrn_gates(hf, llb, l1m):
    c = l1m + _log_sigmoid(hf)
    mx = jnp.maximum(llb, c)
    logf = mx + jnp.log1p(jnp.exp(-jnp.abs(llb - c)))
    k = jnp.exp(l1m + _log_sigmoid(-hf))
    return logf, k


def _cumsum_rows(x, chunk):
    r = x.shape[0]
    ri = lax.broadcasted_iota(I32, (r, r), 0)
    ci = lax.broadcasted_iota(I32, (r, r), 1)
    sh = int(math.log2(chunk))
    tri = ((lax.shift_right_logical(ri, sh) == lax.shift_right_logical(ci, sh)) & (ci <= ri))
    return jnp.dot(tri.astype(F32), x, precision=lax.Precision.HIGHEST, preferred_element_type=F32)


def _hgrn_state0_body(hf_ref, hi_ref, llb_ref, l1m_ref, s_ref):
    logf, k = _hgrn_gates(hf_ref[...], llb_ref[...], l1m_ref[...])
    n = logf.shape[0]
    b = _cumsum_rows(logf, n)
    kd = k * jnp.exp(b[n - 1:n, :] - b)
    s_ref[0] = _dot(hi_ref[...].astype(F32).T.astype(BF16), kd.astype(BF16))


def _hgrn_state0(hf_m, hi_m, llb, l1m, nh):
    n, w = hf_m.shape
    d = w // nh
    col = lambda h: (0, h)
    return pl.pallas_call(
        _hgrn_state0_body,
        grid=(nh,),
        in_specs=[pl.BlockSpec((n, d), col), pl.BlockSpec((n, d), col),
                  pl.BlockSpec((1, d), col), pl.BlockSpec((1, d), col)],
        out_specs=pl.BlockSpec((1, d, d), lambda h: (h, 0, 0)),
        out_shape=jax.ShapeDtypeStruct((nh, d, d), F32),
        compiler_params=_params(("parallel",)),
        name="hgrn_state0",
    )(hf_m, hi_m, llb, l1m)


def _hgrn_body(hf_ref, hq_ref, hi_ref, hg_ref, llb_ref, l1m_ref, ng_ref, s0_ref, o_ref,
               st_sc, oc_sc, *, chunk, sub):
    r, d = hf_ref.shape

    @pl.when(pl.program_id(2) == 0)
    def _():
        st_sc[...] = s0_ref[0]

    logf, k = _hgrn_gates(hf_ref[...], llb_ref[...], l1m_ref[...])
    hq = hq_ref[...].astype(F32)
    q = hq * _sigmoid(hq)
    v = hi_ref[...].astype(F32)
    b = _cumsum_rows(logf, chunk)

    ones = jnp.ones((d, d), BF16)
    row_in_sub = lax.broadcasted_iota(I32, (r, 1), 0) & (sub - 1)
    o_diag = jnp.zeros((r, d), F32)
    kr, br, vr = k, b, v
    for off in range(sub):
        if off:
            kr = pltpu.roll(kr, 1, 0)
            br = pltpu.roll(br, 1, 0)
            vr = pltpu.roll(vr, 1, 0)
        arg = jnp.where(row_in_sub >= off, b - br, NEG_INF)
        prod = q * kr * jnp.exp(arg)
        hi = prod.astype(BF16)
        lo = (prod - hi.astype(F32)).astype(BF16)
        o_diag = o_diag + (_dot(hi, ones) + _dot(lo, ones)) * vr

    srow = lax.broadcasted_iota(I32, (chunk, 1), 0)
    vb = v.astype(BF16)
    for c in range(r // chunk):
        cs = slice(c * chunk, (c + 1) * chunk)
        qc, kc, bc, vc = q[cs], k[cs], b[cs], vb[cs]
        st = st_sc[...]
        blast = bc[chunk - 1:chunk, :]
        parts = [_nt_dot((qc * jnp.exp(bc)).astype(BF16), st.astype(BF16))]
        off_parts = [jnp.zeros((sub, d), F32)]
        for i in range(1, chunk // sub):
            anc = bc[i * sub - 1:i * sub, :]
            ss = slice(i * sub, (i + 1) * sub)
            q_i = (qc[ss] * jnp.exp(bc[ss] - anc)).astype(BF16)
            k_i = (kc * jnp.exp(jnp.where(srow < i * sub, anc - bc, NEG_INF))).astype(BF16)
            a_i = _nt_dot(q_i, k_i)
            off_parts.append(_dot(a_i.astype(BF16), vc))
        oc_sc[cs, :] = parts[0] + jnp.concatenate(off_parts, axis=0)
        kd = (kc * jnp.exp(blast - bc)).astype(BF16)
        st_sc[...] = jnp.exp(blast) * st + _dot(vc.astype(F32).T.astype(BF16), kd)

    o = oc_sc[...] + o_diag
    o = _rms_norm(o, ng_ref[...])
    hg = hg_ref[...].astype(F32)
    o_ref[...] = (o * (hg * _sigmoid(hg))).astype(o_ref.dtype)


def _hgrn(hf, qig, llb, l1m, norm_g, s0t, batch, seq, rows):
    m, w = hf.shape
    nh = s0t.shape[0]
    d = w // nh
    nblk = seq // rows
    assert seq % rows == 0 and rows % HG_CHUNK == 0 and m == batch * seq
    body = functools.partial(_hgrn_body, chunk=HG_CHUNK, sub=HG_SUB)
    blk = lambda g: pl.BlockSpec((rows, d), lambda b, h, c, g=g: (b * nblk + c, g * nh + h))
    vec = pl.BlockSpec((1, d), lambda b, h, c: (0, h))
    return pl.pallas_call(
        body,
        grid=(batch, nh, nblk),
        in_specs=[blk(0), blk(0), blk(1), blk(2), vec, vec, vec,
                  pl.BlockSpec((1, d, d), lambda b, h, c: (h, 0, 0))],
        out_specs=blk(0),
        out_shape=jax.ShapeDtypeStruct((m, w), BF16),
        scratch_shapes=[pltpu.VMEM((d, d), F32), pltpu.VMEM((rows, d), F32)],
        compiler_params=_params(("parallel", "parallel", "arbitrary")),
        name="hgrn",
    )(hf, qig, qig, qig, llb, l1m, norm_g, s0t)


def _ln1_body(h_ref, mix_ref, g_ref, b_ref, wr_ref, br_ref, o32_ref, o16_ref, lg_ref):
    y = _layer_norm(DN_ALPHA * h_ref[...] + mix_ref[...], g_ref[...], b_ref[...])
    o32_ref[...] = y
    o16_ref[...] = y.astype(BF16)
    lg_ref[...] = jnp.dot(y, wr_ref[...], precision=lax.Precision.HIGHEST,
                          preferred_element_type=F32) + br_ref[...]


def _ln1_router(h, mix, g, b, w_router, b_router, tm):
    m, d = h.shape
    ne = w_router.shape[1]
    row = pl.BlockSpec((tm, d), lambda i: (i, 0))
    vec = pl.BlockSpec((1, d), lambda i: (0, 0))
    return pl.pallas_call(
        _ln1_body,
        grid=(pl.cdiv(m, tm),),
        in_specs=[row, row, vec, vec,
                  pl.BlockSpec((d, ne), lambda i: (0, 0)), pl.BlockSpec((1, ne), lambda i: (0, 0))],
        out_specs=[row, row, pl.BlockSpec((tm, ne), lambda i: (i, 0))],
        out_shape=[jax.ShapeDtypeStruct((m, d), F32), jax.ShapeDtypeStruct((m, d), BF16),
                   jax.ShapeDtypeStruct((m, ne), F32)],
        compiler_params=_params(("parallel",)),
        name="ln1_router",
    )(h, mix, g.reshape(1, d), b.reshape(1, d), w_router, b_router.reshape(1, ne))


def _moe_up_body(be_ref, nb_ref, x_ref, wg_ref, bg_ref, wu_ref, bu_ref, o_ref):
    @pl.when(pl.program_id(1) < nb_ref[0])
    def _():
        x = x_ref[...]
        a = jnp.minimum(_dot(x, wg_ref[0]) + bg_ref[0], SWIGLU_LIMIT)
        u = jnp.clip(_dot(x, wu_ref[0]) + bu_ref[0], -SWIGLU_LIMIT, SWIGLU_LIMIT)
        o_ref[...] = ((u + 1.0) * a * _sigmoid(SWIGLU_ALPHA * a)).astype(o_ref.dtype)


def _moe_down_body(be_ref, nb_ref, a_ref, wd_ref, bd_ref, g_ref, o_ref):
    @pl.when(pl.program_id(1) < nb_ref[0])
    def _():
        y = _dot(a_ref[...], wd_ref[0]) + bd_ref[0]
        o_ref[...] = (y * g_ref[...]).astype(o_ref.dtype)


def _moe_experts(xs, slot_gate, blk_expert, n_used, w_gate, b_gate, w_up, b_up, w_down, b_down,
                 tn_up, tn_down):
    cap, d = xs.shape
    ne, _, f = w_gate.shape
    nblk = cap // MOE_ROWS
    tn_up = min(tn_up, f)
    tn_down = min(tn_down, d)
    act = pl.pallas_call(
        _moe_up_body,
        grid_spec=pltpu.PrefetchScalarGridSpec(
            num_scalar_prefetch=2,
            grid=(f // tn_up, nblk),
            in_specs=[pl.BlockSpec((MOE_ROWS, d), lambda j, m, be, nb: (m, 0)),
                      pl.BlockSpec((1, d, tn_up), lambda j, m, be, nb: (be[m], 0, j)),
                      pl.BlockSpec((1, 1, tn_up), lambda j, m, be, nb: (be[m], 0, j)),
                      pl.BlockSpec((1, d, tn_up), lambda j, m, be, nb: (be[m], 0, j)),
                      pl.BlockSpec((1, 1, tn_up), lambda j, m, be, nb: (be[m], 0, j))],
            out_specs=pl.BlockSpec((MOE_ROWS, tn_up), lambda j, m, be, nb: (m, j))),
        out_shape=jax.ShapeDtypeStruct((cap, f), BF16),
        compiler_params=_params(("arbitrary", "arbitrary")),
        name="moe_up",
    )(blk_expert, n_used, xs, w_gate, b_gate.reshape(ne, 1, f), w_up, b_up.reshape(ne, 1, f))
    return pl.pallas_call(
        _moe_down_body,
        grid_spec=pltpu.PrefetchScalarGridSpec(
            num_scalar_prefetch=2,
            grid=(d // tn_down, nblk),
            in_specs=[pl.BlockSpec((MOE_ROWS, f), lambda j, m, be, nb: (m, 0)),
                      pl.BlockSpec((1, f, tn_down), lambda j, m, be, nb: (be[m], 0, j)),
                      pl.BlockSpec((1, 1, tn_down), lambda j, m, be, nb: (be[m], 0, j)),
                      pl.BlockSpec((MOE_ROWS, 1), lambda j, m, be, nb: (m, 0))],
            out_specs=pl.BlockSpec((MOE_ROWS, tn_down), lambda j, m, be, nb: (m, j))),
        out_shape=jax.ShapeDtypeStruct((cap, d), BF16),
        compiler_params=_params(("arbitrary", "arbitrary")),
        name="moe_down",
    )(blk_expert, n_used, act, w_down, b_down.reshape(ne, 1, d), slot_gate.reshape(cap, 1))


def _moe_route(logits):
    n, ne = logits.shape
    nk = n * TOP_K
    top_val, top_idx = lax.top_k(logits, TOP_K)
    gate = jax.nn.softmax(top_val, axis=-1).reshape(nk)
    flat_e = top_idx.reshape(nk).astype(I32)
    order = jnp.argsort(flat_e)
    se = flat_e[order]
    counts = jnp.bincount(flat_e, length=ne).astype(I32)
    padded = (counts + MOE_ROWS - 1) // MOE_ROWS * MOE_ROWS
    pad_end = jnp.cumsum(padded)
    pad_start = pad_end - padded
    start = jnp.cumsum(counts) - counts
    dest = pad_start[se] + jnp.arange(nk, dtype=I32) - start[se]
    nblk = -(-(nk + ne * (MOE_ROWS - 1)) // MOE_ROWS)
    cap = nblk * MOE_ROWS
    slot_tok = jnp.zeros((cap,), I32).at[dest].set(order // TOP_K)
    slot_gate = jnp.zeros((cap,), F32).at[dest].set(gate[order])
    pos = jnp.zeros((nk,), I32).at[order].set(dest)
    blk_expert = jnp.minimum(
        jnp.searchsorted(pad_end, jnp.arange(nblk, dtype=I32) * MOE_ROWS, side="right"),
        ne - 1).astype(I32)
    n_used = (pad_end[-1:] // MOE_ROWS).astype(I32)
    return slot_tok, slot_gate, pos.reshape(n, TOP_K), blk_expert, n_used


def _ln2_body(h_ref, y_ref, g_ref, b_ref, o_ref):
    d = h_ref.shape[1]
    ffn = y_ref[:, :d].astype(F32)
    for c in range(1, y_ref.shape[1] // d):
        ffn = ffn + y_ref[:, c * d:(c + 1) * d].astype(F32)
    o_ref[...] = _layer_norm(DN_ALPHA * h_ref[...] + ffn, g_ref[...], b_ref[...])


def _ln2(h, y4, g, b, tm):
    m, d = h.shape
    row = pl.BlockSpec((tm, d), lambda i: (i, 0))
    vec = pl.BlockSpec((1, d), lambda i: (0, 0))
    return pl.pallas_call(
        _ln2_body,
        grid=(pl.cdiv(m, tm),),
        in_specs=[row, pl.BlockSpec((tm, y4.shape[1]), lambda i: (i, 0)), vec, vec],
        out_specs=row,
        out_shape=jax.ShapeDtypeStruct((m, d), F32),
        compiler_params=_params(("parallel",)),
        name="ln2",
    )(h, y4, g.reshape(1, d), b.reshape(1, d))


def _split_w_in(w_in):
    n_dsa = DSA_Q_RANK + DSA_KV_RANK + IDX_DIM + IDX_HEADS
    w_dsa = w_in[:, :n_dsa]
    w_hq, w_hf, w_hi, w_hg = [w_in[:, n_dsa + i * HG_WIDTH:n_dsa + (i + 1) * HG_WIDTH] for i in range(4)]
    w_qig = jnp.concatenate([w_hq, w_hi, w_hg], axis=1)
    return w_dsa.astype(BF16), w_hf.astype(BF16), w_qig.astype(BF16)


def _forward(x, meta_tokens, emb_ln_g, emb_ln_b, lb_logits, w_in, q_norm_g, kv_norm_g, w_uq, w_uk, w_uv,
             w_iq, idx_k_ln_g, idx_k_ln_b, hgrn_norm_g, w_o, ln1_g, ln1_b, w_router, b_router,
             w_gate, b_gate, w_up, b_up, w_down, b_down, ln2_g, ln2_b):
    batch, seq, d = x.shape
    n = batch * seq
    k_sel = min(IDX_TOPK_MAX, seq // 4)

    lb = jnp.cumsum(jax.nn.softmax(lb_logits.astype(F32), axis=0), axis=0)[0]
    llb = jnp.log(lb).reshape(1, HG_WIDTH)
    l1m = jnp.log1p(-lb).reshape(1, HG_WIDTH)
    w_dsa, w_hf, w_qig = _split_w_in(w_in)
    w_uq3 = w_uq.reshape(DSA_Q_RANK, DSA_HEADS, DSA_HEAD_DIM).transpose(1, 0, 2).astype(BF16)
    w_ukt3 = w_uk.transpose(1, 2, 0).astype(BF16)
    w_uv3 = w_uv.transpose(1, 0, 2).astype(BF16)
    w_iq3 = w_iq.reshape(DSA_Q_RANK, IDX_HEADS, IDX_DIM).transpose(1, 0, 2).astype(BF16)

    def mixer_inputs(tokens, tm):
        h32, h16 = _ln_rows(tokens, emb_ln_g, emb_ln_b, 256)
        dsa_in = _matmul(h16, w_dsa, F32, 512, w_dsa.shape[1], "proj_dsa")
        hf = _matmul(h16, w_hf, F32, tm, 512, "proj_hf")
        qig = _matmul(h16, w_qig, BF16, tm, 512, "proj_qig")
        prep = _dsa_prep(dsa_in, q_norm_g, kv_norm_g, idx_k_ln_g, idx_k_ln_b, w_uq3, w_ukt3, w_iq3, 512)
        return h32, hf, qig, prep

    _, hf_m, qig_m, (_, _, ckv_m, _, _) = mixer_inputs(meta_tokens.astype(F32), N_META)
    s0t = _hgrn_state0(hf_m, qig_m[:, HG_WIDTH:2 * HG_WIDTH], llb, l1m, HG_HEADS)

    h0, hf, qig, (qa, iq, ckv, ikn, iwt) = mixer_inputs(x.reshape(n, d), 1024)
    a = _dsa_attention(qa, iq, iwt, ikn, ckv, ckv_m, w_uv3, batch, seq, k_sel, 128, 512)
    g = _hgrn(hf, qig, llb, l1m, hgrn_norm_g.reshape(1, HG_WIDTH), s0t, batch, seq, 512)

    mix = _matmul(jnp.concatenate([a, g], axis=1), w_o.astype(BF16), F32, 1024, 512, "proj_out")
    h1, h1b, logits = _ln1_router(h0, mix, ln1_g, ln1_b, w_router, b_router, 256)

    slot_tok, slot_gate, pos, blk_expert, n_used = _moe_route(logits)
    xs = jnp.take(h1b, slot_tok, axis=0)
    y = _moe_experts(xs, slot_gate, blk_expert, n_used, w_gate.astype(BF16), b_gate, w_up.astype(BF16), b_up,
                     w_down.astype(BF16), b_down, 768, 1024)
    y4 = jnp.take(y, pos.reshape(n * TOP_K), axis=0).reshape(n, TOP_K * d)
    out = _ln2(h1, y4, ln2_g, ln2_b, 128)
    return out.reshape(batch, seq, d)


def kernel(x, meta_tokens, emb_ln_g, emb_ln_b, lb_logits, w_in, q_norm_g, kv_norm_g, w_uq, w_uk, w_uv, w_iq, idx_k_ln_g, idx_k_ln_b, hgrn_norm_g, w_o, ln1_g, ln1_b, w_router, b_router, w_gate, b_gate, w_up, b_up, w_down, b_down, ln2_g, ln2_b):
    return _forward(x, meta_tokens, emb_ln_g, emb_ln_b, lb_logits, w_in[0], q_norm_g[0], kv_norm_g[0],
                    w_uq[0], w_uk[0], w_uv[0], w_iq[0], idx_k_ln_g[0], idx_k_ln_b[0], hgrn_norm_g[0],
                    w_o[0], ln1_g[0], ln1_b[0], w_router[0], b_router[0], w_gate[0], b_gate[0],
                    w_up[0], b_up[0], w_down[0], b_down[0], ln2_g[0], ln2_b[0])
```

```python
import functools
import math

import jax
import jax.numpy as jnp
from jax import lax
from jax.experimental import pallas as pl
from jax.experimental.pallas import tpu as pltpu

F32 = jnp.float32
BF16 = jnp.bfloat16
I32 = jnp.int32

N_META = 16
DSA_HEADS = 16
DSA_HEAD_DIM = 128
DSA_Q_RANK = 768
DSA_KV_RANK = 512
IDX_HEADS = 16
IDX_DIM = 64
IDX_TOPK_MAX = 256
HG_HEADS = 16
HG_DK = 128
HG_DV = 128
HG_WIDTH = HG_HEADS * HG_DK
HG_CHUNK = 64
HG_SUB = 8
N_EXPERTS = 32
TOP_K = 4
SWIGLU_LIMIT = 7.0
SWIGLU_ALPHA = 1.702
LN_EPS = 1e-5
RMS_EPS = 1e-6
DEPTH = 1
DN_ALPHA = (2 * DEPTH) ** 0.25

V7X_VMEM_LIMIT_BYTES = 56 * 1024 * 1024
MOE_ROWS = 512
INT_MIN = -(2 ** 31)
NEG_INF = float("-inf")
LOG2E = 1.4426950408889634


def _params(sem):
    return pltpu.CompilerParams(dimension_semantics=sem, vmem_limit_bytes=V7X_VMEM_LIMIT_BYTES)


def _nt_dot(a, b):
    return lax.dot_general(a, b, (((1,), (1,)), ((), ())), preferred_element_type=F32)


def _dot(a, b):
    return jnp.dot(a, b, preferred_element_type=F32)


def _layer_norm(x, g, b):
    mu = jnp.mean(x, axis=-1, keepdims=True)
    xc = x - mu
    var = jnp.mean(xc * xc, axis=-1, keepdims=True)
    return xc * lax.rsqrt(var + LN_EPS) * g + b


def _rms_norm(x, g):
    ms = jnp.mean(x * x, axis=-1, keepdims=True)
    return x * lax.rsqrt(ms + RMS_EPS) * g


def _sigmoid(x):
    return 0.5 * jnp.tanh(0.5 * x) + 0.5


def _ln_body(x_ref, g_ref, b_ref, o32_ref, o16_ref):
    y = _layer_norm(x_ref[...].astype(F32), g_ref[...], b_ref[...])
    o32_ref[...] = y
    o16_ref[...] = y.astype(BF16)


def _ln_rows(x, g, b, tm):
    m, d = x.shape
    tm = min(tm, m)
    row = pl.BlockSpec((tm, d), lambda i: (i, 0))
    vec = pl.BlockSpec((1, d), lambda i: (0, 0))
    return pl.pallas_call(
        _ln_body,
        grid=(pl.cdiv(m, tm),),
        in_specs=[row, vec, vec],
        out_specs=[row, row],
        out_shape=[jax.ShapeDtypeStruct((m, d), F32), jax.ShapeDtypeStruct((m, d), BF16)],
        compiler_params=_params(("parallel",)),
        name="embed_ln",
    )(x, g.reshape(1, d), b.reshape(1, d))


def _mm_body(a_ref, b_ref, o_ref):
    o_ref[...] = _dot(a_ref[...], b_ref[...]).astype(o_ref.dtype)


def _matmul(a, b, out_dtype, tm, tn, name):
    m, k = a.shape
    n = b.shape[1]
    tm = min(tm, m)
    tn = min(tn, n)
    return pl.pallas_call(
        _mm_body,
        grid=(pl.cdiv(m, tm), pl.cdiv(n, tn)),
        in_specs=[pl.BlockSpec((tm, k), lambda i, j: (i, 0)),
                  pl.BlockSpec((k, tn), lambda i, j: (0, j))],
        out_specs=pl.BlockSpec((tm, tn), lambda i, j: (i, j)),
        out_shape=jax.ShapeDtypeStruct((m, n), out_dtype),
        compiler_params=_params(("parallel", "arbitrary")),
        name=name,
    )(a, b)


def _dsa_prep_body(x_ref, qg_ref, kvg_ref, ikg_ref, ikb_ref, wuq_ref, wukt_ref, wiq_ref,
                   qa_ref, iq_ref, ckv_ref, ikn_ref, iwt_ref, cq_sc, *, iw_scale):
    h = pl.program_id(1)
    rq = qg_ref.shape[1]
    rkv = kvg_ref.shape[1]
    di = ikg_ref.shape[1]

    @pl.when(h == 0)
    def _():
        x = x_ref[...]
        cq_sc[...] = _rms_norm(x[:, :rq], qg_ref[...]).astype(BF16)
        ckv_ref[...] = _rms_norm(x[:, rq:rq + rkv], kvg_ref[...]).astype(BF16)
        ik = x[:, rq + rkv:rq + rkv + di]
        ikn_ref[...] = _layer_norm(ik, ikg_ref[...], ikb_ref[...]).astype(BF16)
        iwt_ref[...] = x[:, rq + rkv + di:] * iw_scale

    cq = cq_sc[...]
    q = _dot(cq, wuq_ref[0]).astype(BF16)
    qa_ref[0] = _dot(q, wukt_ref[0]).astype(BF16)
    iq_ref[0] = _dot(cq, wiq_ref[0]).astype(BF16)


def _dsa_prep(x, q_g, kv_g, ik_g, ik_b, w_uq3, w_ukt3, w_iq3, tm):
    m, w = x.shape
    nh, rq, dh = w_uq3.shape
    rkv = w_ukt3.shape[2]
    di = w_iq3.shape[2]
    nih = w - rq - rkv - di
    assert nih == nh
    tm = min(tm, m)
    vec = lambda n: pl.BlockSpec((1, n), lambda i, h: (0, 0))
    return pl.pallas_call(
        functools.partial(_dsa_prep_body, iw_scale=nih ** -0.5 * di ** -0.5),
        grid=(pl.cdiv(m, tm), nh),
        in_specs=[pl.BlockSpec((tm, w), lambda i, h: (i, 0)),
                  vec(rq), vec(rkv), vec(di), vec(di),
                  pl.BlockSpec((1, rq, dh), lambda i, h: (h, 0, 0)),
                  pl.BlockSpec((1, dh, rkv), lambda i, h: (h, 0, 0)),
                  pl.BlockSpec((1, rq, di), lambda i, h: (h, 0, 0))],
        out_specs=[pl.BlockSpec((1, tm, rkv), lambda i, h: (h, i, 0)),
                   pl.BlockSpec((1, tm, di), lambda i, h: (h, i, 0)),
                   pl.BlockSpec((tm, rkv), lambda i, h: (i, 0)),
                   pl.BlockSpec((tm, di), lambda i, h: (i, 0)),
                   pl.BlockSpec((tm, nih), lambda i, h: (i, 0))],
        out_shape=[jax.ShapeDtypeStruct((nh, m, rkv), BF16),
                   jax.ShapeDtypeStruct((nh, m, di), BF16),
                   jax.ShapeDtypeStruct((m, rkv), BF16),
                   jax.ShapeDtypeStruct((m, di), BF16),
                   jax.ShapeDtypeStruct((m, nih), F32)],
        scratch_shapes=[pltpu.VMEM((tm, rq), BF16)],
        compiler_params=_params(("parallel", "arbitrary")),
        name="dsa_prep",
    )(x, q_g.reshape(1, rq), kv_g.reshape(1, rkv), ik_g.reshape(1, di), ik_b.reshape(1, di),
      w_uq3, w_ukt3, w_iq3)


def _dsa_attn_body(qa_ref, iq_ref, iwt_ref, ik_ref, ckv_ref, ckvm_ref, wuv_ref, o_ref,
                   key_sc, m_sc, l_sc, al_sc, acc_sc, p_sc, sa_sc, sb_sc, *, k_sel, n_meta, scale, slopes):
    nh, tq, rkv = qa_ref.shape
    di = iq_ref.shape[2]
    nkb_all, _, kblk = key_sc.shape
    mpad = ckvm_ref.shape[0]
    dh = wuv_ref.shape[2]
    rows = nh * tq
    t0 = pl.program_id(1) * tq
    t_col = t0 + lax.broadcasted_iota(I32, (tq, 1), 0)
    hs = [slice(h * tq, (h + 1) * tq) for h in range(nh)]
    nkb = (t0 + tq - 1) // kblk + 1
    c_scale = scale * LOG2E
    c_slope = [s * LOG2E for s in slopes]

    iq2 = iq_ref[...].reshape(rows, di)
    iw = iwt_ref[...]
    for cb in range(nkb_all):
        @pl.when(cb < nkb)
        def _():
            sc = _nt_dot(iq2, ik_ref[cb * kblk:(cb + 1) * kblk, :])
            tot = jnp.zeros((tq, kblk), F32)
            for h in range(nh):
                tot = tot + iw[:, h:h + 1] * jnp.maximum(sc[hs[h]], 0.0)
            bits = lax.bitcast_convert_type(tot, I32)
            key = jnp.where(bits < 0, bits ^ jnp.int32(0x7FFFFFFF), bits)
            col = cb * kblk + lax.broadcasted_iota(I32, (1, kblk), 1)
            key_sc[cb] = jnp.where(col <= t_col, key, jnp.int32(INT_MIN))

    lanes = 128

    def over_blocks(combine, elem, init):
        def step(cb, a):
            e = elem(key_sc[cb])
            for c in range(kblk // lanes):
                a = combine(a, e[:, c * lanes:(c + 1) * lanes])
            return a
        return lax.fori_loop(0, nkb, step, jnp.full((tq, lanes), init))

    def count_ge(cand):
        part = over_blocks(jnp.add, lambda kc: (kc >= cand).astype(F32), jnp.float32(0.0))
        return jnp.sum(part, axis=1, keepdims=True)

    kmax = jnp.max(over_blocks(jnp.maximum, lambda kc: kc, jnp.int32(INT_MIN)), axis=1, keepdims=True)
    lo0 = jnp.full((tq, 1), INT_MIN + 1, I32)
    open0 = (t_col + 1 > k_sel) & (kmax > lo0)
    first_probe = jnp.where(kmax > INT_MIN + 1 + (1 << 24), kmax - (1 << 24), lo0 + 1)

    def bisect_cond(st):
        return (st[0] < 48) & (st[4] > 0.0)

    def bisect_step(st):
        it, lo, hi, open_f, _ = st
        is_open = open_f > 0.0
        mid = jnp.where(it == 0, first_probe, (lo | hi) - lax.shift_right_arithmetic(lo ^ hi, 1))
        cnt = count_ge(mid)
        ge = cnt >= float(k_sel)
        lo = jnp.where(is_open & ge, mid, lo)
        hi = jnp.where(is_open & ge & (cnt == float(k_sel)), mid, jnp.where(is_open & ~ge, mid - 1, hi))
        open_f = (is_open & (lo < hi)).astype(F32)
        return it + 1, lo, hi, open_f, jnp.sum(open_f)

    open_f0 = open0.astype(F32)
    thr = lax.while_loop(bisect_cond, bisect_step, (jnp.int32(0), lo0, kmax, open_f0, jnp.sum(open_f0)))[1]

    qa2 = qa_ref[...].reshape(rows, rkv)
    ckvm = ckvm_ref[...]
    sm = _nt_dot(qa2, ckvm)
    jm = lax.broadcasted_iota(I32, (1, mpad), 1)
    dist_m = (t_col + n_meta - jm).astype(F32)
    for h in range(nh):
        s_h = jnp.where(jm < n_meta, sm[hs[h]] * c_scale - c_slope[h] * dist_m, NEG_INF)
        m_h = jnp.max(s_h, axis=1, keepdims=True)
        p = jnp.exp2(s_h - m_h)
        m_sc[hs[h]] = m_h
        l_sc[hs[h]] = jnp.sum(p, axis=1, keepdims=True)
        p_sc[hs[h], :mpad] = p.astype(BF16)
    acc_sc[...] = _dot(p_sc[:, :mpad], ckvm)

    def kv_block(kb):
        return ckv_ref[pl.ds(pl.multiple_of(kb * kblk, kblk), kblk), :]

    def qk_into(dst, kb):
        dst[...] = _nt_dot(qa2, kv_block(jnp.minimum(kb, nkb - 1)))

    def softmax_pv(src, kb):
        col = kb * kblk + lax.broadcasted_iota(I32, (1, kblk), 1)
        dist = (t_col - col).astype(F32)
        neg = jnp.where(key_sc[kb] >= thr, 0.0, NEG_INF)
        for h in range(nh):
            s_h = src[hs[h], :] * c_scale + (neg - c_slope[h] * dist)
            m_old = m_sc[hs[h]]
            m_new = jnp.maximum(m_old, jnp.max(s_h, axis=1, keepdims=True))
            alpha = jnp.exp2(m_old - m_new)
            p = jnp.exp2(s_h - m_new)
            l_sc[hs[h]] = alpha * l_sc[hs[h]] + jnp.sum(p, axis=1, keepdims=True)
            m_sc[hs[h]] = m_new
            al_sc[hs[h]] = alpha
            p_sc[hs[h]] = p.astype(BF16)
        acc_sc[...] = acc_sc[...] * al_sc[...] + _dot(p_sc[...], kv_block(kb))

    qk_into(sa_sc, 0)

    def two_blocks(i, carry):
        kb = 2 * i
        qk_into(sb_sc, kb + 1)
        softmax_pv(sa_sc, kb)

        @pl.when(kb + 1 < nkb)
        def _():
            qk_into(sa_sc, kb + 2)
            softmax_pv(sb_sc, kb + 1)
        return carry

    lax.fori_loop(0, (nkb + 1) // 2, two_blocks, 0)

    for h in range(nh):
        lat = (acc_sc[hs[h]] / l_sc[hs[h]]).astype(BF16)
        o_ref[:, h * dh:(h + 1) * dh] = _dot(lat, wuv_ref[h]).astype(o_ref.dtype)


def _dsa_attention(qa, iq, iwt, ikn, ckv, ckv_m, w_uv3, batch, seq, k_sel, tq, kblk):
    nh, m, rkv = qa.shape
    di = iq.shape[2]
    dh = w_uv3.shape[2]
    n_meta = ckv_m.shape[0]
    mpad = 128
    assert n_meta <= mpad <= kblk and seq % kblk == 0 and seq % tq == 0 and m == batch * seq
    ckv_mp = jnp.zeros((mpad, rkv), BF16).at[:n_meta].set(ckv_m)
    nq = seq // tq
    slopes = tuple(2.0 ** (-8.0 * (h + 1) / nh) for h in range(nh))
    body = functools.partial(_dsa_attn_body, k_sel=k_sel, n_meta=n_meta,
                             scale=dh ** -0.5, slopes=slopes)
    return pl.pallas_call(
        body,
        grid=(batch, nq),
        in_specs=[pl.BlockSpec((nh, tq, rkv), lambda b, q: (0, b * nq + q, 0)),
                  pl.BlockSpec((nh, tq, di), lambda b, q: (0, b * nq + q, 0)),
                  pl.BlockSpec((tq, nh), lambda b, q: (b * nq + q, 0)),
                  pl.BlockSpec((seq, di), lambda b, q: (b, 0)),
                  pl.BlockSpec((seq, rkv), lambda b, q: (b, 0)),
                  pl.BlockSpec((mpad, rkv), lambda b, q: (0, 0)),
                  pl.BlockSpec((nh, rkv, dh), lambda b, q: (0, 0, 0))],
        out_specs=pl.BlockSpec((tq, nh * dh), lambda b, q: (b * nq + q, 0)),
        out_shape=jax.ShapeDtypeStruct((m, nh * dh), BF16),
        scratch_shapes=[pltpu.VMEM((seq // kblk, tq, kblk), I32),
                        pltpu.VMEM((nh * tq, 1), F32),
                        pltpu.VMEM((nh * tq, 1), F32),
                        pltpu.VMEM((nh * tq, 1), F32),
                        pltpu.VMEM((nh * tq, rkv), F32),
                        pltpu.VMEM((nh * tq, kblk), BF16),
                        pltpu.VMEM((nh * tq, kblk), F32),
                        pltpu.VMEM((nh * tq, kblk), F32)],
        compiler_params=_params(("parallel", "arbitrary")),
        name="dsa_attention",
    )(qa, iq, iwt, ikn, ckv, ckv_mp, w_uv3)


def _hgrn_gates(hf, llb, l1m):
    softplus = jnp.log(1.0 + jnp.exp(-jnp.abs(hf)))
    c = l1m + (jnp.minimum(hf, 0.0) - softplus)
    mx = jnp.maximum(llb, c)
    logf = mx + jnp.log(1.0 + jnp.exp(-jnp.abs(llb - c)))
    k = jnp.exp(l1m + (jnp.minimum(-hf, 0.0) - softplus))
    return logf, k


def _lower_tri(n):
    ri = lax.broadcasted_iota(I32, (n, n), 0)
    ci = lax.broadcasted_iota(I32, (n, n), 1)
    return (ci <= ri).astype(BF16)


def _cumsum_rows(x, tri):
    d = x.shape[1]
    t1 = x.astype(BF16)
    r1 = x - t1.astype(F32)
    t2 = r1.astype(BF16)
    t3 = (r1 - t2.astype(F32)).astype(BF16)
    y = _dot(tri, jnp.concatenate([t1, t2, t3], axis=1))
    return y[:, :d] + y[:, d:2 * d] + y[:, 2 * d:]


def _hgrn_state0_body(hf_ref, hi_ref, llb_ref, l1m_ref, s_ref):
    logf, k = _hgrn_gates(hf_ref[...], llb_ref[...], l1m_ref[...])
    n = logf.shape[0]
    b = _cumsum_rows(logf, _lower_tri(n))
    kd = k * jnp.exp(b[n - 1:n, :] - b)
    s_ref[0] = _dot(hi_ref[...].astype(F32).T.astype(BF16), kd.astype(BF16))


def _hgrn_state0(hf_m, hi_m, llb, l1m, nh):
    n, w = hf_m.shape
    d = w // nh
    col = lambda h: (0, h)
    return pl.pallas_call(
        _hgrn_state0_body,
        grid=(nh,),
        in_specs=[pl.BlockSpec((n, d), col), pl.BlockSpec((n, d), col),
                  pl.BlockSpec((1, d), col), pl.BlockSpec((1, d), col)],
        out_specs=pl.BlockSpec((1, d, d), lambda h: (h, 0, 0)),
        out_shape=jax.ShapeDtypeStruct((nh, d, d), F32),
        compiler_params=_params(("parallel",)),
        name="hgrn_state0",
    )(hf_m, hi_m, llb, l1m)


def _hgrn_body(hf_ref, hq_ref, hi_ref, hg_ref, llb_ref, l1m_ref, ng_ref, s0_ref, o_ref,
               st_sc, oc_sc, *, chunk, sub):
    r, d = hf_ref.shape

    @pl.when(pl.program_id(2) == 0)
    def _():
        st_sc[...] = s0_ref[0]

    logf, k = _hgrn_gates(hf_ref[...], llb_ref[...], l1m_ref[...])
    hq = hq_ref[...].astype(F32)
    q = hq * _sigmoid(hq)
    v = hi_ref[...].astype(F32)
    tri = _lower_tri(chunk)
    nchunk = r // chunk
    b = jnp.concatenate([_cumsum_rows(logf[c * chunk:(c + 1) * chunk], tri) for c in range(nchunk)],
                        axis=0)

    ones = jnp.ones((d, d), BF16)
    q3, k3, b3, v3 = [a.reshape(r // sub, sub, d) for a in (q, k, b, v)]
    row_in_sub = lax.broadcasted_iota(I32, (1, sub, 1), 1)
    o_diag = _dot((q * k).astype(BF16), ones) * v
    for off in range(1, sub):
        kr, br, vr = [pltpu.roll(a, off, 1) for a in (k3, b3, v3)]
        decay = jnp.exp(jnp.where(row_in_sub >= off, b3 - br, NEG_INF))
        prod = (q3 * kr * decay).reshape(r, d).astype(BF16)
        o_diag = o_diag + _dot(prod, ones) * vr.reshape(r, d)

    rowi = lax.broadcasted_iota(I32, (chunk, 1), 0)
    ri = lax.broadcasted_iota(I32, (chunk, chunk), 0)
    ci = lax.broadcasted_iota(I32, (chunk, chunk), 1)
    halves = []
    m = chunk // 2
    while m >= sub:
        sh = int(math.log2(2 * m))
        same_group = lax.shift_right_logical(ri, sh) == lax.shift_right_logical(ci, sh)
        halves.append((m, (rowi & (2 * m - 1)) >= m, same_group))
        m //= 2

    vb = v.astype(BF16)
    for c in range(nchunk):
        cs = slice(c * chunk, (c + 1) * chunk)
        qc, kc, bc, vc = q[cs], k[cs], b[cs], vb[cs]
        st = st_sc[...]
        blast = bc[chunk - 1:chunk, :]
        attn = jnp.zeros((chunk, chunk), F32)
        for m, upper, same_group in halves:
            anc = jnp.concatenate([jnp.broadcast_to(bc[g0 + m - 1:g0 + m, :], (2 * m, d))
                                   for g0 in range(0, chunk, 2 * m)], axis=0)
            q_m = (qc * jnp.exp(jnp.where(upper, bc - anc, NEG_INF))).astype(BF16)
            k_m = (kc * jnp.exp(jnp.where(upper, NEG_INF, anc - bc))).astype(BF16)
            attn = attn + jnp.where(same_group, _nt_dot(q_m, k_m), 0.0)
        oc_sc[cs, :] = (_nt_dot((qc * jnp.exp(bc)).astype(BF16), st.astype(BF16))
                        + _dot(attn.astype(BF16), vc))
        kd = (kc * jnp.exp(blast - bc)).astype(BF16)
        st_sc[...] = jnp.exp(blast) * st + _dot(vc.astype(F32).T.astype(BF16), kd)

    o = oc_sc[...] + o_diag
    o = _rms_norm(o, ng_ref[...])
    hg = hg_ref[...].astype(F32)
    o_ref[...] = (o * (hg * _sigmoid(hg))).astype(o_ref.dtype)


def _hgrn(hf, qig, llb, l1m, norm_g, s0t, batch, seq, rows):
    m, w = hf.shape
    nh = s0t.shape[0]
    d = w // nh
    nblk = seq // rows
    assert seq % rows == 0 and rows % HG_CHUNK == 0 and m == batch * seq
    body = functools.partial(_hgrn_body, chunk=HG_CHUNK, sub=HG_SUB)
    blk = lambda g: pl.BlockSpec((rows, d), lambda b, h, c, g=g: (b * nblk + c, g * nh + h))
    vec = pl.BlockSpec((1, d), lambda b, h, c: (0, h))
    return pl.pallas_call(
        body,
        grid=(batch, nh, nblk),
        in_specs=[blk(0), blk(0), blk(1), blk(2), vec, vec, vec,
                  pl.BlockSpec((1, d, d), lambda b, h, c: (h, 0, 0))],
        out_specs=blk(0),
        out_shape=jax.ShapeDtypeStruct((m, w), BF16),
        scratch_shapes=[pltpu.VMEM((d, d), F32), pltpu.VMEM((rows, d), F32)],
        compiler_params=_params(("parallel", "parallel", "arbitrary")),
        name="hgrn",
    )(hf, qig, qig, qig, llb, l1m, norm_g, s0t)


def _ln1_body(h_ref, mix_ref, g_ref, b_ref, wr_ref, br_ref, o32_ref, o16_ref, lg_ref):
    y = _layer_norm(DN_ALPHA * h_ref[...] + mix_ref[...], g_ref[...], b_ref[...])
    o32_ref[...] = y
    o16_ref[...] = y.astype(BF16)
    lg_ref[...] = jnp.dot(y, wr_ref[...], precision=lax.Precision.HIGHEST,
                          preferred_element_type=F32) + br_ref[...]


def _ln1_router(h, mix, g, b, w_router, b_router, tm):
    m, d = h.shape
    ne = w_router.shape[1]
    row = pl.BlockSpec((tm, d), lambda i: (i, 0))
    vec = pl.BlockSpec((1, d), lambda i: (0, 0))
    return pl.pallas_call(
        _ln1_body,
        grid=(pl.cdiv(m, tm),),
        in_specs=[row, row, vec, vec,
                  pl.BlockSpec((d, ne), lambda i: (0, 0)), pl.BlockSpec((1, ne), lambda i: (0, 0))],
        out_specs=[row, row, pl.BlockSpec((tm, ne), lambda i: (i, 0))],
        out_shape=[jax.ShapeDtypeStruct((m, d), F32), jax.ShapeDtypeStruct((m, d), BF16),
                   jax.ShapeDtypeStruct((m, ne), F32)],
        compiler_params=_params(("parallel",)),
        name="ln1_router",
    )(h, mix, g.reshape(1, d), b.reshape(1, d), w_router, b_router.reshape(1, ne))


def _expert_changed(be_ref):
    m = pl.program_id(1)
    return (m == 0) | (be_ref[m] != be_ref[jnp.maximum(m - 1, 0)])


def _moe_up_body(be_ref, nb_ref, x_ref, wg_ref, bg_ref, wu_ref, bu_ref, o_ref, wg_sc, wu_sc):
    @pl.when(_expert_changed(be_ref))
    def _():
        wg_sc[...] = wg_ref[0].astype(BF16)
        wu_sc[...] = wu_ref[0].astype(BF16)

    @pl.when(pl.program_id(1) < nb_ref[0])
    def _():
        x = x_ref[...]
        a = jnp.minimum(_dot(x, wg_sc[...]) + bg_ref[0], SWIGLU_LIMIT)
        u = jnp.clip(_dot(x, wu_sc[...]) + bu_ref[0], -SWIGLU_LIMIT, SWIGLU_LIMIT)
        o_ref[...] = ((u + 1.0) * a * _sigmoid(SWIGLU_ALPHA * a)).astype(o_ref.dtype)


def _moe_down_body(be_ref, nb_ref, a_ref, wd_ref, bd_ref, g_ref, o_ref, wd_sc):
    @pl.when(_expert_changed(be_ref))
    def _():
        wd_sc[...] = wd_ref[0].astype(BF16)

    @pl.when(pl.program_id(1) < nb_ref[0])
    def _():
        y = _dot(a_ref[...], wd_sc[...]) + bd_ref[0]
        o_ref[...] = (y * g_ref[...]).astype(o_ref.dtype)


def _moe_experts(xs, slot_gate, blk_expert, n_used, w_gate, b_gate, w_up, b_up, w_down, b_down,
                 tn_up, tn_down):
    cap, d = xs.shape
    ne, _, f = w_gate.shape
    nblk = cap // MOE_ROWS
    tn_up = min(tn_up, f)
    tn_down = min(tn_down, d)
    act = pl.pallas_call(
        _moe_up_body,
        grid_spec=pltpu.PrefetchScalarGridSpec(
            num_scalar_prefetch=2,
            grid=(f // tn_up, nblk),
            in_specs=[pl.BlockSpec((MOE_ROWS, d), lambda j, m, be, nb: (m, 0)),
                      pl.BlockSpec((1, d, tn_up), lambda j, m, be, nb: (be[m], 0, j)),
                      pl.BlockSpec((1, 1, tn_up), lambda j, m, be, nb: (be[m], 0, j)),
                      pl.BlockSpec((1, d, tn_up), lambda j, m, be, nb: (be[m], 0, j)),
                      pl.BlockSpec((1, 1, tn_up), lambda j, m, be, nb: (be[m], 0, j))],
            out_specs=pl.BlockSpec((MOE_ROWS, tn_up), lambda j, m, be, nb: (m, j)),
            scratch_shapes=[pltpu.VMEM((d, tn_up), BF16), pltpu.VMEM((d, tn_up), BF16)]),
        out_shape=jax.ShapeDtypeStruct((cap, f), BF16),
        compiler_params=_params(("arbitrary", "arbitrary")),
        name="moe_up",
    )(blk_expert, n_used, xs, w_gate, b_gate.reshape(ne, 1, f), w_up, b_up.reshape(ne, 1, f))
    return pl.pallas_call(
        _moe_down_body,
        grid_spec=pltpu.PrefetchScalarGridSpec(
            num_scalar_prefetch=2,
            grid=(d // tn_down, nblk),
            in_specs=[pl.BlockSpec((MOE_ROWS, f), lambda j, m, be, nb: (m, 0)),
                      pl.BlockSpec((1, f, tn_down), lambda j, m, be, nb: (be[m], 0, j)),
                      pl.BlockSpec((1, 1, tn_down), lambda j, m, be, nb: (be[m], 0, j)),
                      pl.BlockSpec((MOE_ROWS, 1), lambda j, m, be, nb: (m, 0))],
            out_specs=pl.BlockSpec((MOE_ROWS, tn_down), lambda j, m, be, nb: (m, j)),
            scratch_shapes=[pltpu.VMEM((f, tn_down), BF16)]),
        out_shape=jax.ShapeDtypeStruct((cap, d), BF16),
        compiler_params=_params(("arbitrary", "arbitrary")),
        name="moe_down",
    )(blk_expert, n_used, act, w_down, b_down.reshape(ne, 1, d), slot_gate.reshape(cap, 1))


def _moe_route(logits):
    n, ne = logits.shape
    nk = n * TOP_K
    top_val, top_idx = lax.top_k(logits, TOP_K)
    gate = jax.nn.softmax(top_val, axis=-1).reshape(nk)
    flat_e = top_idx.reshape(nk).astype(I32)
    onehot = (flat_e[:, None] == jnp.arange(ne, dtype=I32)[None, :]).astype(I32)
    csum = jnp.cumsum(onehot, axis=0)
    rank = jnp.sum(csum * onehot, axis=1) - 1
    counts = csum[-1]
    padded = (counts + MOE_ROWS - 1) // MOE_ROWS * MOE_ROWS
    pad_end = jnp.cumsum(padded)
    pos = (pad_end - padded)[flat_e] + rank
    nblk = -(-(nk + ne * (MOE_ROWS - 1)) // MOE_ROWS)
    cap = nblk * MOE_ROWS
    slot_pair = jnp.full((cap,), -1, I32).at[pos].set(jnp.arange(nk, dtype=I32), unique_indices=True)
    filled = slot_pair >= 0
    slot_pair = jnp.maximum(slot_pair, 0)
    slot_tok = slot_pair // TOP_K
    slot_gate = jnp.where(filled, gate[slot_pair], 0.0)
    blk_expert = jnp.minimum(
        jnp.searchsorted(pad_end, jnp.arange(nblk, dtype=I32) * MOE_ROWS, side="right"),
        ne - 1).astype(I32)
    n_used = (pad_end[-1:] // MOE_ROWS).astype(I32)
    return slot_tok, slot_gate, pos.reshape(n, TOP_K), blk_expert, n_used


def _ln2_body(h_ref, y_ref, g_ref, b_ref, o_ref):
    ffn = y_ref[0].astype(F32)
    for c in range(1, y_ref.shape[0]):
        ffn = ffn + y_ref[c].astype(F32)
    o_ref[...] = _layer_norm(DN_ALPHA * h_ref[...] + ffn, g_ref[...], b_ref[...])


def _ln2(h, y4, g, b, tm):
    m, d = h.shape
    row = pl.BlockSpec((tm, d), lambda i: (i, 0))
    vec = pl.BlockSpec((1, d), lambda i: (0, 0))
    return pl.pallas_call(
        _ln2_body,
        grid=(pl.cdiv(m, tm),),
        in_specs=[row, pl.BlockSpec((y4.shape[0], tm, d), lambda i: (0, i, 0)), vec, vec],
        out_specs=row,
        out_shape=jax.ShapeDtypeStruct((m, d), F32),
        compiler_params=_params(("parallel",)),
        name="ln2",
    )(h, y4, g.reshape(1, d), b.reshape(1, d))


def _split_w_in(w_in):
    n_dsa = DSA_Q_RANK + DSA_KV_RANK + IDX_DIM + IDX_HEADS
    w_dsa = w_in[:, :n_dsa]
    w_hq, w_hf, w_hi, w_hg = [w_in[:, n_dsa + i * HG_WIDTH:n_dsa + (i + 1) * HG_WIDTH] for i in range(4)]
    w_qig = jnp.concatenate([w_hq, w_hi, w_hg], axis=1)
    return w_dsa.astype(BF16), w_hf.astype(BF16), w_qig.astype(BF16)


def _forward(x, meta_tokens, emb_ln_g, emb_ln_b, lb_logits, w_in, q_norm_g, kv_norm_g, w_uq, w_uk, w_uv,
             w_iq, idx_k_ln_g, idx_k_ln_b, hgrn_norm_g, w_o, ln1_g, ln1_b, w_router, b_router,
             w_gate, b_gate, w_up, b_up, w_down, b_down, ln2_g, ln2_b):
    batch, seq, d = x.shape
    n = batch * seq
    k_sel = min(IDX_TOPK_MAX, seq // 4)

    lb = jnp.cumsum(jax.nn.softmax(lb_logits.astype(F32), axis=0), axis=0)[0]
    llb = jnp.log(lb).reshape(1, HG_WIDTH)
    l1m = jnp.log1p(-lb).reshape(1, HG_WIDTH)
    w_dsa, w_hf, w_qig = _split_w_in(w_in)
    w_uq3 = w_uq.reshape(DSA_Q_RANK, DSA_HEADS, DSA_HEAD_DIM).transpose(1, 0, 2).astype(BF16)
    w_ukt3 = w_uk.transpose(1, 2, 0).astype(BF16)
    w_uv3 = w_uv.transpose(1, 0, 2).astype(BF16)
    w_iq3 = w_iq.reshape(DSA_Q_RANK, IDX_HEADS, IDX_DIM).transpose(1, 0, 2).astype(BF16)

    def mixer_inputs(tokens, tm):
        h32, h16 = _ln_rows(tokens, emb_ln_g, emb_ln_b, 256)
        dsa_in = _matmul(h16, w_dsa, F32, 512, w_dsa.shape[1], "proj_dsa")
        hf = _matmul(h16, w_hf, F32, tm, 512, "proj_hf")
        qig = _matmul(h16, w_qig, BF16, tm, 512, "proj_qig")
        prep = _dsa_prep(dsa_in, q_norm_g, kv_norm_g, idx_k_ln_g, idx_k_ln_b, w_uq3, w_ukt3, w_iq3, 512)
        return h32, hf, qig, prep

    _, hf_m, qig_m, (_, _, ckv_m, _, _) = mixer_inputs(meta_tokens.astype(F32), N_META)
    s0t = _hgrn_state0(hf_m, qig_m[:, HG_WIDTH:2 * HG_WIDTH], llb, l1m, HG_HEADS)

    h0, hf, qig, (qa, iq, ckv, ikn, iwt) = mixer_inputs(x.reshape(n, d), 1024)
    a = _dsa_attention(qa, iq, iwt, ikn, ckv, ckv_m, w_uv3, batch, seq, k_sel, 128, 512)
    g = _hgrn(hf, qig, llb, l1m, hgrn_norm_g.reshape(1, HG_WIDTH), s0t, batch, seq, 512)

    mix = _matmul(jnp.concatenate([a, g], axis=1), w_o.astype(BF16), F32, 1024, 512, "proj_out")
    h1, h1b, logits = _ln1_router(h0, mix, ln1_g, ln1_b, w_router, b_router, 256)

    slot_tok, slot_gate, pos, blk_expert, n_used = _moe_route(logits)
    xs = jnp.take(h1b, slot_tok, axis=0, mode="clip")
    y = _moe_experts(xs, slot_gate, blk_expert, n_used, w_gate, b_gate, w_up, b_up, w_down, b_down, 384, 2048)
    y4 = jnp.take(y, pos.T.reshape(TOP_K * n), axis=0, mode="clip").reshape(TOP_K, n, d)
    out = _ln2(h1, y4, ln2_g, ln2_b, 128)
    return out.reshape(batch, seq, d)


def kernel(x, meta_tokens, emb_ln_g, emb_ln_b, lb_logits, w_in, q_norm_g, kv_norm_g, w_uq, w_uk, w_uv, w_iq, idx_k_ln_g, idx_k_ln_b, hgrn_norm_g, w_o, ln1_g, ln1_b, w_router, b_router, w_gate, b_gate, w_up, b_up, w_down, b_down, ln2_g, ln2_b):
    return _forward(x, meta_tokens, emb_ln_g, emb_ln_b, lb_logits, w_in[0], q_norm_g[0], kv_norm_g[0],
                    w_uq[0], w_uk[0], w_uv[0], w_iq[0], idx_k_ln_g[0], idx_k_ln_b[0], hgrn_norm_g[0],
                    w_o[0], ln1_g[0], ln1_b[0], w_router[0], b_router[0], w_gate[0], b_gate[0],
                    w_up[0], b_up[0], w_down[0], b_down[0], ln2_g[0], ln2_b[0])
```

```python
import functools
import math

import jax
import jax.numpy as jnp
from jax import lax
from jax.experimental import pallas as pl
from jax.experimental.pallas import tpu as pltpu

F32 = jnp.float32
BF16 = jnp.bfloat16
I32 = jnp.int32

N_META = 16
DSA_HEADS = 16
DSA_HEAD_DIM = 128
DSA_Q_RANK = 768
DSA_KV_RANK = 512
IDX_HEADS = 16
IDX_DIM = 64
IDX_TOPK_MAX = 256
HG_HEADS = 16
HG_DK = 128
HG_DV = 128
HG_WIDTH = HG_HEADS * HG_DK
HG_CHUNK = 64
HG_SUB = 8
N_EXPERTS = 32
TOP_K = 4
SWIGLU_LIMIT = 7.0
SWIGLU_ALPHA = 1.702
LN_EPS = 1e-5
RMS_EPS = 1e-6
DEPTH = 1
DN_ALPHA = (2 * DEPTH) ** 0.25

V7X_VMEM_LIMIT_BYTES = 56 * 1024 * 1024
MOE_ROWS = 512
INT_MIN = -(2 ** 31)
NEG_INF = float("-inf")
LOG2E = 1.4426950408889634


def _params(sem):
    return pltpu.CompilerParams(dimension_semantics=sem, vmem_limit_bytes=V7X_VMEM_LIMIT_BYTES)


def _nt_dot(a, b):
    return lax.dot_general(a, b, (((1,), (1,)), ((), ())), preferred_element_type=F32)


def _dot(a, b):
    return jnp.dot(a, b, preferred_element_type=F32)


def _layer_norm(x, g, b):
    mu = jnp.mean(x, axis=-1, keepdims=True)
    xc = x - mu
    var = jnp.mean(xc * xc, axis=-1, keepdims=True)
    return xc * lax.rsqrt(var + LN_EPS) * g + b


def _rms_norm(x, g):
    ms = jnp.mean(x * x, axis=-1, keepdims=True)
    return x * lax.rsqrt(ms + RMS_EPS) * g


def _sigmoid(x):
    return 0.5 * jnp.tanh(0.5 * x) + 0.5


def _ln_body(x_ref, g_ref, b_ref, o32_ref, o16_ref):
    y = _layer_norm(x_ref[...].astype(F32), g_ref[...], b_ref[...])
    o32_ref[...] = y
    o16_ref[...] = y.astype(BF16)


def _ln_rows(x, g, b, tm):
    m, d = x.shape
    tm = min(tm, m)
    row = pl.BlockSpec((tm, d), lambda i: (i, 0))
    vec = pl.BlockSpec((1, d), lambda i: (0, 0))
    return pl.pallas_call(
        _ln_body,
        grid=(pl.cdiv(m, tm),),
        in_specs=[row, vec, vec],
        out_specs=[row, row],
        out_shape=[jax.ShapeDtypeStruct((m, d), F32), jax.ShapeDtypeStruct((m, d), BF16)],
        compiler_params=_params(("parallel",)),
        name="embed_ln",
    )(x, g.reshape(1, d), b.reshape(1, d))


def _mm_body(a_ref, b_ref, o_ref):
    o_ref[...] = _dot(a_ref[...], b_ref[...]).astype(o_ref.dtype)


def _matmul(a, b, out_dtype, tm, tn, name):
    m, k = a.shape
    n = b.shape[1]
    tm = min(tm, m)
    tn = min(tn, n)
    return pl.pallas_call(
        _mm_body,
        grid=(pl.cdiv(m, tm), pl.cdiv(n, tn)),
        in_specs=[pl.BlockSpec((tm, k), lambda i, j: (i, 0)),
                  pl.BlockSpec((k, tn), lambda i, j: (0, j))],
        out_specs=pl.BlockSpec((tm, tn), lambda i, j: (i, j)),
        out_shape=jax.ShapeDtypeStruct((m, n), out_dtype),
        compiler_params=_params(("parallel", "arbitrary")),
        name=name,
    )(a, b)


def _dsa_prep_body(x_ref, qg_ref, kvg_ref, ikg_ref, ikb_ref, wuq_ref, wukt_ref, wiq_ref,
                   qa_ref, iq_ref, ckv_ref, ikn_ref, iwt_ref, *, iw_scale):
    rq = qg_ref.shape[1]
    rkv = kvg_ref.shape[1]
    di = ikg_ref.shape[1]
    nh, dh, _ = wukt_ref.shape
    x = x_ref[...]
    cq = _rms_norm(x[:, :rq], qg_ref[...]).astype(BF16)
    ckv_ref[...] = _rms_norm(x[:, rq:rq + rkv], kvg_ref[...]).astype(BF16)
    ik = x[:, rq + rkv:rq + rkv + di]
    ikn_ref[...] = _layer_norm(ik, ikg_ref[...], ikb_ref[...]).astype(BF16)
    iwt_ref[...] = x[:, rq + rkv + di:] * iw_scale
    q = _dot(cq, wuq_ref[...]).astype(BF16)
    iq = _dot(cq, wiq_ref[...]).astype(BF16)
    for h in range(nh):
        qa_ref[h] = _dot(q[:, h * dh:(h + 1) * dh], wukt_ref[h]).astype(BF16)
        iq_ref[h] = iq[:, h * di:(h + 1) * di]


def _dsa_prep(x, q_g, kv_g, ik_g, ik_b, w_uq, w_ukt3, w_iq, tm):
    m, w = x.shape
    nh, dh, rkv = w_ukt3.shape
    rq = w_uq.shape[0]
    di = w_iq.shape[1] // nh
    nih = w - rq - rkv - di
    assert nih == nh and w_uq.shape[1] == nh * dh
    tm = min(tm, m)
    vec = lambda n: pl.BlockSpec((1, n), lambda i: (0, 0))
    full = lambda a: pl.BlockSpec(a.shape, lambda i: (0,) * a.ndim)
    return pl.pallas_call(
        functools.partial(_dsa_prep_body, iw_scale=nih ** -0.5 * di ** -0.5),
        grid=(pl.cdiv(m, tm),),
        in_specs=[pl.BlockSpec((tm, w), lambda i: (i, 0)),
                  vec(rq), vec(rkv), vec(di), vec(di), full(w_uq), full(w_ukt3), full(w_iq)],
        out_specs=[pl.BlockSpec((nh, tm, rkv), lambda i: (0, i, 0)),
                   pl.BlockSpec((nh, tm, di), lambda i: (0, i, 0)),
                   pl.BlockSpec((tm, rkv), lambda i: (i, 0)),
                   pl.BlockSpec((tm, di), lambda i: (i, 0)),
                   pl.BlockSpec((tm, nih), lambda i: (i, 0))],
        out_shape=[jax.ShapeDtypeStruct((nh, m, rkv), BF16),
                   jax.ShapeDtypeStruct((nh, m, di), BF16),
                   jax.ShapeDtypeStruct((m, rkv), BF16),
                   jax.ShapeDtypeStruct((m, di), BF16),
                   jax.ShapeDtypeStruct((m, nih), F32)],
        compiler_params=_params(("parallel",)),
        name="dsa_prep",
    )(x, q_g.reshape(1, rq), kv_g.reshape(1, rkv), ik_g.reshape(1, di), ik_b.reshape(1, di),
      w_uq, w_ukt3, w_iq)


def _dsa_attn_body(qa_ref, iq_ref, iwt_ref, ik_ref, ckv_ref, ckvm_ref, wuv_ref, o_ref,
                   key_sc, m_sc, l_sc, al_sc, acc_sc, p_sc, sa_sc, sb_sc, *, k_sel, n_meta, scale, slopes):
    nh, tq, rkv = qa_ref.shape
    di = iq_ref.shape[2]
    nkb_all, _, kblk = key_sc.shape
    mpad = ckvm_ref.shape[0]
    dh = wuv_ref.shape[2]
    rows = nh * tq
    t0 = pl.program_id(1) * tq
    t_col = t0 + lax.broadcasted_iota(I32, (tq, 1), 0)
    hs = [slice(h * tq, (h + 1) * tq) for h in range(nh)]
    nkb = (t0 + tq - 1) // kblk + 1
    c_scale = scale * LOG2E
    c_slope = [s * LOG2E for s in slopes]

    iq2 = iq_ref[...].reshape(rows, di)
    iw = iwt_ref[...]
    for cb in range(nkb_all):
        @pl.when(cb < nkb)
        def _():
            sc = _nt_dot(iq2, ik_ref[cb * kblk:(cb + 1) * kblk, :])
            tot = jnp.zeros((tq, kblk), F32)
            for h in range(nh):
                tot = tot + iw[:, h:h + 1] * jnp.maximum(sc[hs[h]], 0.0)
            bits = lax.bitcast_convert_type(tot, I32)
            key = jnp.where(bits < 0, bits ^ jnp.int32(0x7FFFFFFF), bits)
            col = cb * kblk + lax.broadcasted_iota(I32, (1, kblk), 1)
            key_sc[cb] = jnp.where(col <= t_col, key, jnp.int32(INT_MIN))

    lanes = 128

    def over_blocks(combine, elem, init):
        def step(cb, a):
            e = elem(key_sc[cb])
            for c in range(kblk // lanes):
                a = combine(a, e[:, c * lanes:(c + 1) * lanes])
            return a
        return lax.fori_loop(0, nkb, step, jnp.full((tq, lanes), init))

    def count_ge(cand):
        part = over_blocks(jnp.add, lambda kc: (kc >= cand).astype(F32), jnp.float32(0.0))
        return jnp.sum(part, axis=1, keepdims=True)

    kmax = jnp.max(over_blocks(jnp.maximum, lambda kc: kc, jnp.int32(INT_MIN)), axis=1, keepdims=True)
    lo0 = jnp.full((tq, 1), INT_MIN + 1, I32)
    open0 = (t_col + 1 > k_sel) & (kmax > lo0)
    first_probe = jnp.where(kmax > INT_MIN + 1 + (1 << 24), kmax - (1 << 24), lo0 + 1)

    def probe(lo, hi, open_f, mid):
        is_open = open_f > 0.0
        cnt = count_ge(mid)
        ge = cnt >= float(k_sel)
        lo = jnp.where(is_open & ge, mid, lo)
        hi = jnp.where(is_open & ge & (cnt == float(k_sel)), mid, jnp.where(is_open & ~ge, mid - 1, hi))
        return lo, hi, (is_open & (lo < hi)).astype(F32)

    def bisect_cond(st):
        return (st[0] < 12) & (st[4] > 0.0)

    def bisect_trip(st):
        it, lo, hi, open_f, _ = st
        for _ in range(4):
            lo, hi, open_f = probe(lo, hi, open_f, (lo | hi) - lax.shift_right_arithmetic(lo ^ hi, 1))
        return it + 1, lo, hi, open_f, jnp.sum(open_f)

    lo1, hi1, open_f1 = probe(lo0, kmax, open0.astype(F32), first_probe)
    thr = lax.while_loop(bisect_cond, bisect_trip, (jnp.int32(0), lo1, hi1, open_f1, jnp.sum(open_f1)))[1]

    qa2 = qa_ref[...].reshape(rows, rkv)
    ckvm = ckvm_ref[...]
    sm = _nt_dot(qa2, ckvm)
    jm = lax.broadcasted_iota(I32, (1, mpad), 1)
    pos_m = (jm - n_meta - t0).astype(F32)
    for h in range(nh):
        s_h = jnp.where(jm < n_meta, sm[hs[h]] * c_scale + c_slope[h] * pos_m, NEG_INF)
        m_h = jnp.max(s_h, axis=1, keepdims=True)
        p = jnp.exp2(s_h - m_h)
        m_sc[hs[h]] = m_h
        l_sc[hs[h]] = jnp.sum(p, axis=1, keepdims=True)
        p_sc[hs[h], :mpad] = p.astype(BF16)
    acc_sc[...] = _dot(p_sc[:, :mpad], ckvm)

    def kv_block(kb):
        return ckv_ref[pl.ds(pl.multiple_of(kb * kblk, kblk), kblk), :]

    def qk_into(dst, kb):
        dst[...] = _nt_dot(qa2, kv_block(jnp.minimum(kb, nkb - 1)))

    def softmax_pv(src, kb):
        pos = (kb * kblk - t0 + lax.broadcasted_iota(I32, (1, kblk), 1)).astype(F32)
        neg = jnp.where(key_sc[kb] >= thr, 0.0, NEG_INF)
        for h in range(nh):
            s_h = src[hs[h], :] * c_scale + (neg + c_slope[h] * pos)
            m_old = m_sc[hs[h]]
            m_new = jnp.maximum(m_old, jnp.max(s_h, axis=1, keepdims=True))
            alpha = jnp.exp2(m_old - m_new)
            p = jnp.exp2(s_h - m_new)
            l_sc[hs[h]] = alpha * l_sc[hs[h]] + jnp.sum(p, axis=1, keepdims=True)
            m_sc[hs[h]] = m_new
            al_sc[hs[h]] = alpha
            p_sc[hs[h]] = p.astype(BF16)
        acc_sc[...] = acc_sc[...] * al_sc[...] + _dot(p_sc[...], kv_block(kb))

    qk_into(sa_sc, 0)

    def two_blocks(i, carry):
        kb = 2 * i
        qk_into(sb_sc, kb + 1)
        softmax_pv(sa_sc, kb)

        @pl.when(kb + 1 < nkb)
        def _():
            qk_into(sa_sc, kb + 2)
            softmax_pv(sb_sc, kb + 1)
        return carry

    lax.fori_loop(0, (nkb + 1) // 2, two_blocks, 0)

    inv_l = 1.0 / l_sc[...]
    for h in range(nh):
        out_h = _dot(acc_sc[hs[h]].astype(BF16), wuv_ref[h]) * inv_l[hs[h]]
        o_ref[:, h * dh:(h + 1) * dh] = out_h.astype(o_ref.dtype)


def _dsa_attention(qa, iq, iwt, ikn, ckv, ckv_m, w_uv3, batch, seq, k_sel, tq, kblk):
    nh, m, rkv = qa.shape
    di = iq.shape[2]
    dh = w_uv3.shape[2]
    n_meta = ckv_m.shape[0]
    mpad = 128
    assert n_meta <= mpad <= kblk and seq % kblk == 0 and seq % tq == 0 and m == batch * seq
    ckv_mp = jnp.zeros((mpad, rkv), BF16).at[:n_meta].set(ckv_m)
    nq = seq // tq
    slopes = tuple(2.0 ** (-8.0 * (h + 1) / nh) for h in range(nh))
    body = functools.partial(_dsa_attn_body, k_sel=k_sel, n_meta=n_meta,
                             scale=dh ** -0.5, slopes=slopes)
    return pl.pallas_call(
        body,
        grid=(batch, nq),
        in_specs=[pl.BlockSpec((nh, tq, rkv), lambda b, q: (0, b * nq + q, 0)),
                  pl.BlockSpec((nh, tq, di), lambda b, q: (0, b * nq + q, 0)),
                  pl.BlockSpec((tq, nh), lambda b, q: (b * nq + q, 0)),
                  pl.BlockSpec((seq, di), lambda b, q: (b, 0)),
                  pl.BlockSpec((seq, rkv), lambda b, q: (b, 0)),
                  pl.BlockSpec((mpad, rkv), lambda b, q: (0, 0)),
                  pl.BlockSpec((nh, rkv, dh), lambda b, q: (0, 0, 0))],
        out_specs=pl.BlockSpec((tq, nh * dh), lambda b, q: (b * nq + q, 0)),
        out_shape=jax.ShapeDtypeStruct((m, nh * dh), BF16),
        scratch_shapes=[pltpu.VMEM((seq // kblk, tq, kblk), I32),
                        pltpu.VMEM((nh * tq, 1), F32),
                        pltpu.VMEM((nh * tq, 1), F32),
                        pltpu.VMEM((nh * tq, 1), F32),
                        pltpu.VMEM((nh * tq, rkv), F32),
                        pltpu.VMEM((nh * tq, kblk), BF16),
                        pltpu.VMEM((nh * tq, kblk), F32),
                        pltpu.VMEM((nh * tq, kblk), F32)],
        compiler_params=_params(("parallel", "arbitrary")),
        name="dsa_attention",
    )(qa, iq, iwt, ikn, ckv, ckv_mp, w_uv3)


def _hgrn_gates(hf, llb, l1m):
    softplus = jnp.log(1.0 + jnp.exp(-jnp.abs(hf)))
    c = l1m + (jnp.minimum(hf, 0.0) - softplus)
    mx = jnp.maximum(llb, c)
    logf = mx + jnp.log(1.0 + jnp.exp(-jnp.abs(llb - c)))
    k = jnp.exp(l1m + (jnp.minimum(-hf, 0.0) - softplus))
    return logf, k


def _lower_tri(n):
    ri = lax.broadcasted_iota(I32, (n, n), 0)
    ci = lax.broadcasted_iota(I32, (n, n), 1)
    return (ci <= ri).astype(BF16)


def _cumsum_rows(x, tri):
    d = x.shape[1]
    t1 = x.astype(BF16)
    r1 = x - t1.astype(F32)
    t2 = r1.astype(BF16)
    t3 = (r1 - t2.astype(F32)).astype(BF16)
    y = _dot(tri, jnp.concatenate([t1, t2, t3], axis=1))
    return y[:, :d] + y[:, d:2 * d] + y[:, 2 * d:]


def _hgrn_state0_body(hf_ref, hi_ref, llb_ref, l1m_ref, s_ref):
    logf, k = _hgrn_gates(hf_ref[...], llb_ref[...], l1m_ref[...])
    n = logf.shape[0]
    b = _cumsum_rows(logf, _lower_tri(n))
    kd = k * jnp.exp(b[n - 1:n, :] - b)
    s_ref[0] = _dot(hi_ref[...].astype(F32).T.astype(BF16), kd.astype(BF16))


def _hgrn_state0(hf_m, hi_m, llb, l1m, nh):
    n, w = hf_m.shape
    d = w // nh
    col = lambda h: (0, h)
    return pl.pallas_call(
        _hgrn_state0_body,
        grid=(nh,),
        in_specs=[pl.BlockSpec((n, d), col), pl.BlockSpec((n, d), col),
                  pl.BlockSpec((1, d), col), pl.BlockSpec((1, d), col)],
        out_specs=pl.BlockSpec((1, d, d), lambda h: (h, 0, 0)),
        out_shape=jax.ShapeDtypeStruct((nh, d, d), F32),
        compiler_params=_params(("parallel",)),
        name="hgrn_state0",
    )(hf_m, hi_m, llb, l1m)


def _hgrn_body(hf_ref, hq_ref, hi_ref, hg_ref, llb_ref, l1m_ref, ng_ref, s0_ref, o_ref,
               st_sc, oc_sc, *, chunk, sub):
    r, d = hf_ref.shape

    @pl.when(pl.program_id(2) == 0)
    def _():
        st_sc[...] = s0_ref[0]

    logf, k = _hgrn_gates(hf_ref[...], llb_ref[...], l1m_ref[...])
    hq = hq_ref[...].astype(F32)
    q = hq * _sigmoid(hq)
    v = hi_ref[...].astype(F32)
    tri = _lower_tri(chunk)
    nchunk = r // chunk
    b = jnp.concatenate([_cumsum_rows(logf[c * chunk:(c + 1) * chunk], tri) for c in range(nchunk)],
                        axis=0)

    ones = jnp.ones((d, d), BF16)
    q3, k3, b3, v3 = [a.reshape(r // sub, sub, d) for a in (q, k, b, v)]
    row_in_sub = lax.broadcasted_iota(I32, (1, sub, 1), 1)
    o_diag = _dot((q * k).astype(BF16), ones) * v
    for off in range(1, sub):
        kr, br, vr = [pltpu.roll(a, off, 1) for a in (k3, b3, v3)]
        decay = jnp.exp(jnp.where(row_in_sub >= off, b3 - br, NEG_INF))
        prod = (q3 * kr * decay).reshape(r, d).astype(BF16)
        o_diag = o_diag + _dot(prod, ones) * vr.reshape(r, d)

    rowi = lax.broadcasted_iota(I32, (chunk, 1), 0)
    ri = lax.broadcasted_iota(I32, (chunk, chunk), 0)
    ci = lax.broadcasted_iota(I32, (chunk, chunk), 1)
    halves = []
    m = chunk // 2
    while m >= sub:
        sh = int(math.log2(2 * m))
        same_group = lax.shift_right_logical(ri, sh) == lax.shift_right_logical(ci, sh)
        halves.append((m, (rowi & (2 * m - 1)) >= m, same_group))
        m //= 2

    vb = v.astype(BF16)
    for c in range(nchunk):
        cs = slice(c * chunk, (c + 1) * chunk)
        qc, kc, bc, vc = q[cs], k[cs], b[cs], vb[cs]
        st = st_sc[...]
        blast = bc[chunk - 1:chunk, :]
        attn = jnp.zeros((chunk, chunk), F32)
        for m, upper, same_group in halves:
            anc = jnp.concatenate([jnp.broadcast_to(bc[g0 + m - 1:g0 + m, :], (2 * m, d))
                                   for g0 in range(0, chunk, 2 * m)], axis=0)
            q_m = (qc * jnp.exp(jnp.where(upper, bc - anc, NEG_INF))).astype(BF16)
            k_m = (kc * jnp.exp(jnp.where(upper, NEG_INF, anc - bc))).astype(BF16)
            attn = attn + jnp.where(same_group, _nt_dot(q_m, k_m), 0.0)
        oc_sc[cs, :] = (_nt_dot((qc * jnp.exp(bc)).astype(BF16), st.astype(BF16))
                        + _dot(attn.astype(BF16), vc))
        kd = (kc * jnp.exp(blast - bc)).astype(BF16)
        st_sc[...] = jnp.exp(blast) * st + _dot(vc.astype(F32).T.astype(BF16), kd)

    o = oc_sc[...] + o_diag
    o = _rms_norm(o, ng_ref[...])
    hg = hg_ref[...].astype(F32)
    o_ref[...] = (o * (hg * _sigmoid(hg))).astype(o_ref.dtype)


def _hgrn(hf, qig, llb, l1m, norm_g, s0t, batch, seq, rows):
    m, w = hf.shape
    nh = s0t.shape[0]
    d = w // nh
    nblk = seq // rows
    assert seq % rows == 0 and rows % HG_CHUNK == 0 and m == batch * seq
    body = functools.partial(_hgrn_body, chunk=HG_CHUNK, sub=HG_SUB)
    blk = lambda g: pl.BlockSpec((rows, d), lambda b, h, c, g=g: (b * nblk + c, g * nh + h))
    vec = pl.BlockSpec((1, d), lambda b, h, c: (0, h))
    return pl.pallas_call(
        body,
        grid=(batch, nh, nblk),
        in_specs=[blk(0), blk(0), blk(1), blk(2), vec, vec, vec,
                  pl.BlockSpec((1, d, d), lambda b, h, c: (h, 0, 0))],
        out_specs=blk(0),
        out_shape=jax.ShapeDtypeStruct((m, w), BF16),
        scratch_shapes=[pltpu.VMEM((d, d), F32), pltpu.VMEM((rows, d), F32)],
        compiler_params=_params(("parallel", "parallel", "arbitrary")),
        name="hgrn",
    )(hf, qig, qig, qig, llb, l1m, norm_g, s0t)


def _ln1_body(h_ref, mix_ref, g_ref, b_ref, wr_ref, br_ref, o32_ref, o16_ref, lg_ref):
    y = _layer_norm(DN_ALPHA * h_ref[...] + mix_ref[...], g_ref[...], b_ref[...])
    o32_ref[...] = y
    o16_ref[...] = y.astype(BF16)
    lg_ref[...] = jnp.dot(y, wr_ref[...], precision=lax.Precision.HIGHEST,
                          preferred_element_type=F32) + br_ref[...]


def _ln1_router(h, mix, g, b, w_router, b_router, tm):
    m, d = h.shape
    ne = w_router.shape[1]
    row = pl.BlockSpec((tm, d), lambda i: (i, 0))
    vec = pl.BlockSpec((1, d), lambda i: (0, 0))
    return pl.pallas_call(
        _ln1_body,
        grid=(pl.cdiv(m, tm),),
        in_specs=[row, row, vec, vec,
                  pl.BlockSpec((d, ne), lambda i: (0, 0)), pl.BlockSpec((1, ne), lambda i: (0, 0))],
        out_specs=[row, row, pl.BlockSpec((tm, ne), lambda i: (i, 0))],
        out_shape=[jax.ShapeDtypeStruct((m, d), F32), jax.ShapeDtypeStruct((m, d), BF16),
                   jax.ShapeDtypeStruct((m, ne), F32)],
        compiler_params=_params(("parallel",)),
        name="ln1_router",
    )(h, mix, g.reshape(1, d), b.reshape(1, d), w_router, b_router.reshape(1, ne))


def _expert_changed(be_ref):
    m = pl.program_id(1)
    return (m == 0) | (be_ref[m] != be_ref[jnp.maximum(m - 1, 0)])


def _moe_up_body(be_ref, nb_ref, x_ref, wg_ref, bg_ref, wu_ref, bu_ref, o_ref, wg_sc, wu_sc):
    @pl.when(_expert_changed(be_ref))
    def _():
        wg_sc[...] = wg_ref[0].astype(BF16)
        wu_sc[...] = wu_ref[0].astype(BF16)

    @pl.when(pl.program_id(1) < nb_ref[0])
    def _():
        x = x_ref[...]
        a = jnp.minimum(_dot(x, wg_sc[...]) + bg_ref[0], SWIGLU_LIMIT)
        u = jnp.clip(_dot(x, wu_sc[...]) + bu_ref[0], -SWIGLU_LIMIT, SWIGLU_LIMIT)
        o_ref[...] = ((u + 1.0) * a * _sigmoid(SWIGLU_ALPHA * a)).astype(o_ref.dtype)


def _moe_down_body(be_ref, nb_ref, a_ref, wd_ref, bd_ref, g_ref, o_ref, wd_sc):
    @pl.when(_expert_changed(be_ref))
    def _():
        wd_sc[...] = wd_ref[0].astype(BF16)

    @pl.when(pl.program_id(1) < nb_ref[0])
    def _():
        y = _dot(a_ref[...], wd_sc[...]) + bd_ref[0]
        o_ref[...] = (y * g_ref[...]).astype(o_ref.dtype)


def _moe_experts(xs, slot_gate, blk_expert, n_used, w_gate, b_gate, w_up, b_up, w_down, b_down,
                 tn_up, tn_down):
    cap, d = xs.shape
    ne, _, f = w_gate.shape
    nblk = cap // MOE_ROWS
    tn_up = min(tn_up, f)
    tn_down = min(tn_down, d)
    act = pl.pallas_call(
        _moe_up_body,
        grid_spec=pltpu.PrefetchScalarGridSpec(
            num_scalar_prefetch=2,
            grid=(f // tn_up, nblk),
            in_specs=[pl.BlockSpec((MOE_ROWS, d), lambda j, m, be, nb: (m, 0)),
                      pl.BlockSpec((1, d, tn_up), lambda j, m, be, nb: (be[m], 0, j)),
                      pl.BlockSpec((1, 1, tn_up), lambda j, m, be, nb: (be[m], 0, j)),
                      pl.BlockSpec((1, d, tn_up), lambda j, m, be, nb: (be[m], 0, j)),
                      pl.BlockSpec((1, 1, tn_up), lambda j, m, be, nb: (be[m], 0, j))],
            out_specs=pl.BlockSpec((MOE_ROWS, tn_up), lambda j, m, be, nb: (m, j)),
            scratch_shapes=[pltpu.VMEM((d, tn_up), BF16), pltpu.VMEM((d, tn_up), BF16)]),
        out_shape=jax.ShapeDtypeStruct((cap, f), BF16),
        compiler_params=_params(("arbitrary", "arbitrary")),
        name="moe_up",
    )(blk_expert, n_used, xs, w_gate, b_gate.reshape(ne, 1, f), w_up, b_up.reshape(ne, 1, f))
    return pl.pallas_call(
        _moe_down_body,
        grid_spec=pltpu.PrefetchScalarGridSpec(
            num_scalar_prefetch=2,
            grid=(d // tn_down, nblk),
            in_specs=[pl.BlockSpec((MOE_ROWS, f), lambda j, m, be, nb: (m, 0)),
                      pl.BlockSpec((1, f, tn_down), lambda j, m, be, nb: (be[m], 0, j)),
                      pl.BlockSpec((1, 1, tn_down), lambda j, m, be, nb: (be[m], 0, j)),
                      pl.BlockSpec((MOE_ROWS, 1), lambda j, m, be, nb: (m, 0))],
            out_specs=pl.BlockSpec((MOE_ROWS, tn_down), lambda j, m, be, nb: (m, j)),
            scratch_shapes=[pltpu.VMEM((f, tn_down), BF16)]),
        out_shape=jax.ShapeDtypeStruct((cap, d), BF16),
        compiler_params=_params(("arbitrary", "arbitrary")),
        name="moe_down",
    )(blk_expert, n_used, act, w_down, b_down.reshape(ne, 1, d), slot_gate.reshape(cap, 1))


def _moe_route(logits):
    n, ne = logits.shape
    nk = n * TOP_K
    top_val, top_idx = lax.top_k(logits, TOP_K)
    gate = jax.nn.softmax(top_val, axis=-1).reshape(nk)
    flat_e = top_idx.reshape(nk).astype(I32)
    onehot = (flat_e[:, None] == jnp.arange(ne, dtype=I32)[None, :]).astype(I32)
    csum = jnp.cumsum(onehot, axis=0)
    rank = jnp.sum(csum * onehot, axis=1) - 1
    counts = csum[-1]
    padded = (counts + MOE_ROWS - 1) // MOE_ROWS * MOE_ROWS
    pad_end = jnp.cumsum(padded)
    pos = (pad_end - padded)[flat_e] + rank
    nblk = -(-(nk + ne * (MOE_ROWS - 1)) // MOE_ROWS)
    cap = nblk * MOE_ROWS
    slot_pair = jnp.full((cap,), -1, I32).at[pos].set(jnp.arange(nk, dtype=I32), unique_indices=True)
    filled = slot_pair >= 0
    slot_pair = jnp.where(filled, slot_pair, jnp.arange(cap, dtype=I32) % nk)
    slot_tok = slot_pair // TOP_K
    slot_gate = jnp.where(filled, gate[slot_pair], 0.0)
    blk_expert = jnp.minimum(
        jnp.searchsorted(pad_end, jnp.arange(nblk, dtype=I32) * MOE_ROWS, side="right"),
        ne - 1).astype(I32)
    n_used = (pad_end[-1:] // MOE_ROWS).astype(I32)
    return slot_tok, slot_gate, pos.reshape(n, TOP_K), blk_expert, n_used


def _ln2_body(h_ref, y_ref, g_ref, b_ref, o_ref):
    ffn = y_ref[0].astype(F32)
    for c in range(1, y_ref.shape[0]):
        ffn = ffn + y_ref[c].astype(F32)
    o_ref[...] = _layer_norm(DN_ALPHA * h_ref[...] + ffn, g_ref[...], b_ref[...])


def _ln2(h, y4, g, b, tm):
    m, d = h.shape
    row = pl.BlockSpec((tm, d), lambda i: (i, 0))
    vec = pl.BlockSpec((1, d), lambda i: (0, 0))
    return pl.pallas_call(
        _ln2_body,
        grid=(pl.cdiv(m, tm),),
        in_specs=[row, pl.BlockSpec((y4.shape[0], tm, d), lambda i: (0, i, 0)), vec, vec],
        out_specs=row,
        out_shape=jax.ShapeDtypeStruct((m, d), F32),
        compiler_params=_params(("parallel",)),
        name="ln2",
    )(h, y4, g.reshape(1, d), b.reshape(1, d))


def _split_w_in(w_in):
    n_dsa = DSA_Q_RANK + DSA_KV_RANK + IDX_DIM + IDX_HEADS
    w_dsa = w_in[:, :n_dsa]
    w_hq, w_hf, w_hi, w_hg = [w_in[:, n_dsa + i * HG_WIDTH:n_dsa + (i + 1) * HG_WIDTH] for i in range(4)]
    w_qig = jnp.concatenate([w_hq, w_hi, w_hg], axis=1)
    return w_dsa.astype(BF16), w_hf.astype(BF16), w_qig.astype(BF16)


def _forward(x, meta_tokens, emb_ln_g, emb_ln_b, lb_logits, w_in, q_norm_g, kv_norm_g, w_uq, w_uk, w_uv,
             w_iq, idx_k_ln_g, idx_k_ln_b, hgrn_norm_g, w_o, ln1_g, ln1_b, w_router, b_router,
             w_gate, b_gate, w_up, b_up, w_down, b_down, ln2_g, ln2_b):
    batch, seq, d = x.shape
    n = batch * seq
    k_sel = min(IDX_TOPK_MAX, seq // 4)

    lb = jnp.cumsum(jax.nn.softmax(lb_logits.astype(F32), axis=0), axis=0)[0]
    llb = jnp.log(lb).reshape(1, HG_WIDTH)
    l1m = jnp.log1p(-lb).reshape(1, HG_WIDTH)
    w_dsa, w_hf, w_qig = _split_w_in(w_in)
    w_uqb = w_uq.astype(BF16)
    w_iqb = w_iq.astype(BF16)
    w_ukt3 = w_uk.transpose(1, 2, 0).astype(BF16)
    w_uv3 = w_uv.transpose(1, 0, 2).astype(BF16)

    def mixer_inputs(tokens, tm):
        h32, h16 = _ln_rows(tokens, emb_ln_g, emb_ln_b, 256)
        dsa_in = _matmul(h16, w_dsa, F32, 512, w_dsa.shape[1], "proj_dsa")
        hf = _matmul(h16, w_hf, F32, tm, 512, "proj_hf")
        qig = _matmul(h16, w_qig, BF16, tm, 512, "proj_qig")
        prep = _dsa_prep(dsa_in, q_norm_g, kv_norm_g, idx_k_ln_g, idx_k_ln_b, w_uqb, w_ukt3, w_iqb, 256)
        return h32, hf, qig, prep

    _, hf_m, qig_m, (_, _, ckv_m, _, _) = mixer_inputs(meta_tokens.astype(F32), N_META)
    s0t = _hgrn_state0(hf_m, qig_m[:, HG_WIDTH:2 * HG_WIDTH], llb, l1m, HG_HEADS)

    h0, hf, qig, (qa, iq, ckv, ikn, iwt) = mixer_inputs(x.reshape(n, d), 1024)
    a = _dsa_attention(qa, iq, iwt, ikn, ckv, ckv_m, w_uv3, batch, seq, k_sel, 128, 512)
    g = _hgrn(hf, qig, llb, l1m, hgrn_norm_g.reshape(1, HG_WIDTH), s0t, batch, seq, 1024)

    mix = _matmul(jnp.concatenate([a, g], axis=1), w_o.astype(BF16), F32, 1024, 512, "proj_out")
    h1, h1b, logits = _ln1_router(h0, mix, ln1_g, ln1_b, w_router, b_router, 256)

    slot_tok, slot_gate, pos, blk_expert, n_used = _moe_route(logits)
    xs = jnp.take(h1b, slot_tok, axis=0, mode="clip")
    y = _moe_experts(xs, slot_gate, blk_expert, n_used, w_gate, b_gate, w_up, b_up, w_down, b_down, 256, 2048)
    y4 = jnp.take(y, pos.T.reshape(TOP_K * n), axis=0, mode="clip").reshape(TOP_K, n, d)
    out = _ln2(h1, y4, ln2_g, ln2_b, 128)
    return out.reshape(batch, seq, d)


def kernel(x, meta_tokens, emb_ln_g, emb_ln_b, lb_logits, w_in, q_norm_g, kv_norm_g, w_uq, w_uk, w_uv, w_iq, idx_k_ln_g, idx_k_ln_b, hgrn_norm_g, w_o, ln1_g, ln1_b, w_router, b_router, w_gate, b_gate, w_up, b_up, w_down, b_down, ln2_g, ln2_b):
    return _forward(x, meta_tokens, emb_ln_g, emb_ln_b, lb_logits, w_in[0], q_norm_g[0], kv_norm_g[0],
                    w_uq[0], w_uk[0], w_uv[0], w_iq[0], idx_k_ln_g[0], idx_k_ln_b[0], hgrn_norm_g[0],
                    w_o[0], ln1_g[0], ln1_b[0], w_router[0], b_router[0], w_gate[0], b_gate[0],
                    w_up[0], b_up[0], w_down[0], b_down[0], ln2_g[0], ln2_b[0])
```

```python
import functools
import math

import jax
import jax.numpy as jnp
from jax import lax
from jax.experimental import pallas as pl
from jax.experimental.pallas import tpu as pltpu

F32 = jnp.float32
BF16 = jnp.bfloat16
I32 = jnp.int32

N_META = 16
DSA_HEADS = 16
DSA_HEAD_DIM = 128
DSA_Q_RANK = 768
DSA_KV_RANK = 512
IDX_HEADS = 16
IDX_DIM = 64
IDX_TOPK_MAX = 256
HG_HEADS = 16
HG_DK = 128
HG_DV = 128
HG_WIDTH = HG_HEADS * HG_DK
HG_CHUNK = 64
HG_SUB = 8
N_EXPERTS = 32
TOP_K = 4
SWIGLU_LIMIT = 7.0
SWIGLU_ALPHA = 1.702
LN_EPS = 1e-5
RMS_EPS = 1e-6
DEPTH = 1
DN_ALPHA = (2 * DEPTH) ** 0.25

V7X_VMEM_LIMIT_BYTES = 56 * 1024 * 1024
MOE_ROWS = 512
INT_MIN = -(2 ** 31)
NEG_INF = float("-inf")
LOG2E = 1.4426950408889634


def _params(sem):
    return pltpu.CompilerParams(dimension_semantics=sem, vmem_limit_bytes=V7X_VMEM_LIMIT_BYTES)


def _nt_dot(a, b):
    return lax.dot_general(a, b, (((1,), (1,)), ((), ())), preferred_element_type=F32)


def _dot(a, b):
    return jnp.dot(a, b, preferred_element_type=F32)


def _layer_norm(x, g, b):
    mu = jnp.mean(x, axis=-1, keepdims=True)
    xc = x - mu
    var = jnp.mean(xc * xc, axis=-1, keepdims=True)
    return xc * lax.rsqrt(var + LN_EPS) * g + b


def _rms_norm(x, g):
    ms = jnp.mean(x * x, axis=-1, keepdims=True)
    return x * lax.rsqrt(ms + RMS_EPS) * g


def _sigmoid(x):
    return 0.5 * jnp.tanh(0.5 * x) + 0.5


def _ln_body(x_ref, g_ref, b_ref, o32_ref, o16_ref):
    y = _layer_norm(x_ref[...].astype(F32), g_ref[...], b_ref[...])
    o32_ref[...] = y
    o16_ref[...] = y.astype(BF16)


def _ln_rows(x, g, b, tm):
    m, d = x.shape
    tm = min(tm, m)
    row = pl.BlockSpec((tm, d), lambda i: (i, 0))
    vec = pl.BlockSpec((1, d), lambda i: (0, 0))
    return pl.pallas_call(
        _ln_body,
        grid=(pl.cdiv(m, tm),),
        in_specs=[row, vec, vec],
        out_specs=[row, row],
        out_shape=[jax.ShapeDtypeStruct((m, d), F32), jax.ShapeDtypeStruct((m, d), BF16)],
        compiler_params=_params(("parallel",)),
        name="embed_ln",
    )(x, g.reshape(1, d), b.reshape(1, d))


def _mm_body(a_ref, b_ref, o_ref):
    o_ref[...] = _dot(a_ref[...], b_ref[...]).astype(o_ref.dtype)


def _matmul(a, b, out_dtype, tm, tn, name):
    m, k = a.shape
    n = b.shape[1]
    tm = min(tm, m)
    tn = min(tn, n)
    return pl.pallas_call(
        _mm_body,
        grid=(pl.cdiv(m, tm), pl.cdiv(n, tn)),
        in_specs=[pl.BlockSpec((tm, k), lambda i, j: (i, 0)),
                  pl.BlockSpec((k, tn), lambda i, j: (0, j))],
        out_specs=pl.BlockSpec((tm, tn), lambda i, j: (i, j)),
        out_shape=jax.ShapeDtypeStruct((m, n), out_dtype),
        compiler_params=_params(("parallel", "arbitrary")),
        name=name,
    )(a, b)


def _dsa_prep_body(x_ref, qg_ref, kvg_ref, ikg_ref, ikb_ref, wuq_ref, wukt_ref, wiq_ref,
                   qa_ref, iq_ref, ckv_ref, ikn_ref, iwt_ref, *, iw_scale, qk_scale):
    rq = qg_ref.shape[1]
    rkv = kvg_ref.shape[1]
    di = ikg_ref.shape[1]
    nh, dh, _ = wukt_ref.shape
    x = x_ref[...]
    cq = _rms_norm(x[:, :rq], qg_ref[...]).astype(BF16)
    ckv_ref[...] = _rms_norm(x[:, rq:rq + rkv], kvg_ref[...]).astype(BF16)
    ik = x[:, rq + rkv:rq + rkv + di]
    ikn_ref[...] = _layer_norm(ik, ikg_ref[...], ikb_ref[...]).astype(BF16)
    iwt_ref[...] = x[:, rq + rkv + di:] * iw_scale
    q = _dot(cq, wuq_ref[...]).astype(BF16)
    iq = _dot(cq, wiq_ref[...]).astype(BF16)
    for h in range(nh):
        qa_ref[h] = (_dot(q[:, h * dh:(h + 1) * dh], wukt_ref[h]) * qk_scale).astype(BF16)
        iq_ref[h] = iq[:, h * di:(h + 1) * di]


def _dsa_prep(x, q_g, kv_g, ik_g, ik_b, w_uq, w_ukt3, w_iq, tm):
    m, w = x.shape
    nh, dh, rkv = w_ukt3.shape
    rq = w_uq.shape[0]
    di = w_iq.shape[1] // nh
    nih = w - rq - rkv - di
    assert nih == nh and w_uq.shape[1] == nh * dh
    tm = min(tm, m)
    vec = lambda n: pl.BlockSpec((1, n), lambda i: (0, 0))
    full = lambda a: pl.BlockSpec(a.shape, lambda i: (0,) * a.ndim)
    return pl.pallas_call(
        functools.partial(_dsa_prep_body, iw_scale=nih ** -0.5 * di ** -0.5, qk_scale=dh ** -0.5 * LOG2E),
        grid=(pl.cdiv(m, tm),),
        in_specs=[pl.BlockSpec((tm, w), lambda i: (i, 0)),
                  vec(rq), vec(rkv), vec(di), vec(di), full(w_uq), full(w_ukt3), full(w_iq)],
        out_specs=[pl.BlockSpec((nh, tm, rkv), lambda i: (0, i, 0)),
                   pl.BlockSpec((nh, tm, di), lambda i: (0, i, 0)),
                   pl.BlockSpec((tm, rkv), lambda i: (i, 0)),
                   pl.BlockSpec((tm, di), lambda i: (i, 0)),
                   pl.BlockSpec((tm, nih), lambda i: (i, 0))],
        out_shape=[jax.ShapeDtypeStruct((nh, m, rkv), BF16),
                   jax.ShapeDtypeStruct((nh, m, di), BF16),
                   jax.ShapeDtypeStruct((m, rkv), BF16),
                   jax.ShapeDtypeStruct((m, di), BF16),
                   jax.ShapeDtypeStruct((m, nih), F32)],
        compiler_params=_params(("parallel",)),
        name="dsa_prep",
    )(x, q_g.reshape(1, rq), kv_g.reshape(1, rkv), ik_g.reshape(1, di), ik_b.reshape(1, di),
      w_uq, w_ukt3, w_iq)


def _dsa_attn_body(qa_ref, iq_ref, iwt_ref, ik_ref, ckv_ref, ckvm_ref, wuv_ref, o_ref,
                   key_sc, m_sc, l_sc, al_sc, acc_sc, p_sc, sa_sc, sb_sc, *, k_sel, n_meta, slopes):
    nh, tq, rkv = qa_ref.shape
    di = iq_ref.shape[2]
    nkb_all, _, kblk = key_sc.shape
    mpad = ckvm_ref.shape[0]
    dh = wuv_ref.shape[2]
    rows = nh * tq
    t0 = pl.program_id(1) * tq
    t_col = t0 + lax.broadcasted_iota(I32, (tq, 1), 0)
    hs = [slice(h * tq, (h + 1) * tq) for h in range(nh)]
    nkb = (t0 + tq - 1) // kblk + 1
    c_slope = [s * LOG2E for s in slopes]
    heads_per_group = 4
    slab = tq

    iq2 = iq_ref[...].reshape(rows, di)
    iw = iwt_ref[...]
    for cb in range(nkb_all):
        @pl.when(cb < nkb)
        def _():
            sc = _nt_dot(iq2, ik_ref[cb * kblk:(cb + 1) * kblk, :])
            tot = jnp.zeros((tq, kblk), F32)
            for h in range(nh):
                tot = tot + iw[:, h:h + 1] * jnp.maximum(sc[hs[h]], 0.0)
            bits = lax.bitcast_convert_type(tot, I32)
            key = jnp.where(bits < 0, bits ^ jnp.int32(0x7FFFFFFF), bits)
            col = cb * kblk + lax.broadcasted_iota(I32, (1, kblk), 1)
            key_sc[cb] = jnp.where(col <= t_col, key, jnp.int32(INT_MIN))

    lanes = 128
    half = tq // 2
    halves = (slice(0, half), slice(half, tq))

    def fold(combine, elem, init, nblk, rs):
        a = jnp.full((half, lanes), init)
        for cb in range(nblk):
            e = elem(key_sc[cb, rs, :])
            for c in range(kblk // lanes):
                a = combine(a, e[:, c * lanes:(c + 1) * lanes])
        return a

    def threshold(nblk):
        def count_ge(cand, rs):
            part = fold(jnp.add, lambda kc: (kc >= cand).astype(F32), jnp.float32(0.0), nblk, rs)
            return jnp.sum(part, axis=1, keepdims=True)

        def probe(st, mid, rs):
            lo, hi, open_f = st
            is_open = open_f > 0.0
            cnt = count_ge(mid, rs)
            ge = cnt >= float(k_sel)
            lo = jnp.where(is_open & ge, mid, lo)
            hi = jnp.where(is_open & ge & (cnt == float(k_sel)), mid, jnp.where(is_open & ~ge, mid - 1, hi))
            return lo, hi, (is_open & (lo < hi)).astype(F32)

        def midpoint(st):
            return (st[0] | st[1]) - lax.shift_right_arithmetic(st[0] ^ st[1], 1)

        states = []
        for rs in halves:
            kmax = jnp.max(fold(jnp.maximum, lambda kc: kc, jnp.int32(INT_MIN), nblk, rs), axis=1, keepdims=True)
            lo0 = jnp.full((half, 1), INT_MIN + 1, I32)
            open0 = (t_col[rs] + 1 > k_sel) & (kmax > lo0)
            first = jnp.where(kmax > INT_MIN + 1 + (1 << 24), kmax - (1 << 24), lo0 + 1)
            states.append(probe((lo0, kmax, open0.astype(F32)), first, rs))

        def n_open(sts):
            return jnp.sum(sts[0][2]) + jnp.sum(sts[1][2])

        def trip(carry):
            it, sts, _ = carry
            sts = list(sts)
            for _ in range(4):
                for i, rs in enumerate(halves):
                    sts[i] = probe(sts[i], midpoint(sts[i]), rs)
            return it + 1, tuple(sts), n_open(sts)

        out = lax.while_loop(lambda c: (c[0] < 12) & (c[2] > 0.0), trip,
                             (jnp.int32(0), tuple(states), n_open(states)))
        return jnp.concatenate([out[1][0][0], out[1][1][0]], axis=0)

    thr = lax.switch(nkb - 1, [functools.partial(threshold, c) for c in range(1, nkb_all + 1)])

    qa2 = qa_ref[...].reshape(rows, rkv)
    ckvm = ckvm_ref[...]
    sm = _nt_dot(qa2, ckvm)
    jm = lax.broadcasted_iota(I32, (1, mpad), 1)
    pos_m = (jm - n_meta - t0).astype(F32)
    for h in range(nh):
        s_h = jnp.where(jm < n_meta, sm[hs[h]] + c_slope[h] * pos_m, NEG_INF)
        m_h = jnp.max(s_h, axis=1, keepdims=True)
        p = jnp.exp2(s_h - m_h)
        m_sc[hs[h]] = m_h
        l_sc[hs[h]] = jnp.sum(p, axis=1, keepdims=True)
        p_sc[hs[h], :mpad] = p.astype(BF16)
    acc_sc[...] = _dot(p_sc[:, :mpad], ckvm)

    def kv_block(kb):
        return ckv_ref[pl.ds(pl.multiple_of(kb * kblk, kblk), kblk), :]

    def qk_into(dst, kb):
        dst[...] = _nt_dot(qa2, kv_block(jnp.minimum(kb, nkb - 1)))

    def softmax_pv(src, kb):
        pos = (kb * kblk - t0 + lax.broadcasted_iota(I32, (1, kblk), 1)).astype(F32)
        neg = jnp.where(key_sc[kb] >= thr, 0.0, NEG_INF)
        kv = kv_block(kb)
        m_all = m_sc[...]
        l_all = l_sc[...]
        for g in range(0, nh, heads_per_group):
            for h in range(g, g + heads_per_group):
                bias = neg + c_slope[h] * pos
                for r in range(0, tq, slab):
                    rs = slice(h * tq + r, h * tq + r + slab)
                    s_r = src[rs, :] + bias[r:r + slab]
                    m_old = m_all[rs]
                    m_new = jnp.maximum(m_old, jnp.max(s_r, axis=1, keepdims=True))
                    alpha = jnp.exp2(m_old - m_new)
                    p = jnp.exp2(s_r - m_new)
                    l_sc[rs] = alpha * l_all[rs] + jnp.sum(p, axis=1, keepdims=True)
                    m_sc[rs] = m_new
                    al_sc[rs] = alpha
                    p_sc[rs] = p.astype(BF16)
            gs = slice(g * tq, (g + heads_per_group) * tq)
            acc_sc[gs] = acc_sc[gs] * al_sc[gs] + _dot(p_sc[gs], kv)

    qk_into(sa_sc, 0)

    def two_blocks(i, carry):
        kb = 2 * i
        qk_into(sb_sc, kb + 1)
        softmax_pv(sa_sc, kb)

        @pl.when(kb + 1 < nkb)
        def _():
            qk_into(sa_sc, kb + 2)
            softmax_pv(sb_sc, kb + 1)
        return carry

    lax.fori_loop(0, (nkb + 1) // 2, two_blocks, 0)

    inv_l = 1.0 / l_sc[...]
    for h in range(nh):
        out_h = _dot(acc_sc[hs[h]].astype(BF16), wuv_ref[h]) * inv_l[hs[h]]
        o_ref[:, h * dh:(h + 1) * dh] = out_h.astype(o_ref.dtype)


def _dsa_attention(qa, iq, iwt, ikn, ckv, ckv_m, w_uv3, batch, seq, k_sel, tq, kblk):
    nh, m, rkv = qa.shape
    di = iq.shape[2]
    dh = w_uv3.shape[2]
    n_meta = ckv_m.shape[0]
    mpad = 128
    assert n_meta <= mpad <= kblk and seq % kblk == 0 and seq % tq == 0 and m == batch * seq
    ckv_mp = jnp.zeros((mpad, rkv), BF16).at[:n_meta].set(ckv_m)
    nq = seq // tq
    slopes = tuple(2.0 ** (-8.0 * (h + 1) / nh) for h in range(nh))
    body = functools.partial(_dsa_attn_body, k_sel=k_sel, n_meta=n_meta, slopes=slopes)
    return pl.pallas_call(
        body,
        grid=(batch, nq),
        in_specs=[pl.BlockSpec((nh, tq, rkv), lambda b, q: (0, b * nq + q, 0)),
                  pl.BlockSpec((nh, tq, di), lambda b, q: (0, b * nq + q, 0)),
                  pl.BlockSpec((tq, nh), lambda b, q: (b * nq + q, 0)),
                  pl.BlockSpec((seq, di), lambda b, q: (b, 0)),
                  pl.BlockSpec((seq, rkv), lambda b, q: (b, 0)),
                  pl.BlockSpec((mpad, rkv), lambda b, q: (0, 0)),
                  pl.BlockSpec((nh, rkv, dh), lambda b, q: (0, 0, 0))],
        out_specs=pl.BlockSpec((tq, nh * dh), lambda b, q: (b * nq + q, 0)),
        out_shape=jax.ShapeDtypeStruct((m, nh * dh), BF16),
        scratch_shapes=[pltpu.VMEM((seq // kblk, tq, kblk), I32),
                        pltpu.VMEM((nh * tq, 1), F32),
                        pltpu.VMEM((nh * tq, 1), F32),
                        pltpu.VMEM((nh * tq, 1), F32),
                        pltpu.VMEM((nh * tq, rkv), F32),
                        pltpu.VMEM((nh * tq, kblk), BF16),
                        pltpu.VMEM((nh * tq, kblk), F32),
                        pltpu.VMEM((nh * tq, kblk), F32)],
        compiler_params=_params(("parallel", "arbitrary")),
        name="dsa_attention",
    )(qa, iq, iwt, ikn, ckv, ckv_mp, w_uv3)


def _hgrn_gates(hf, llb, l1m):
    softplus = jnp.log(1.0 + jnp.exp(-jnp.abs(hf)))
    c = l1m + (jnp.minimum(hf, 0.0) - softplus)
    mx = jnp.maximum(llb, c)
    logf = mx + jnp.log(1.0 + jnp.exp(-jnp.abs(llb - c)))
    k = jnp.exp(l1m + (jnp.minimum(-hf, 0.0) - softplus))
    return logf, k


def _lower_tri(n):
    ri = lax.broadcasted_iota(I32, (n, n), 0)
    ci = lax.broadcasted_iota(I32, (n, n), 1)
    return (ci <= ri).astype(BF16)


def _cumsum_rows(x, tri):
    d = x.shape[1]
    t1 = x.astype(BF16)
    r1 = x - t1.astype(F32)
    t2 = r1.astype(BF16)
    t3 = (r1 - t2.astype(F32)).astype(BF16)
    y = _dot(tri, jnp.concatenate([t1, t2, t3], axis=1))
    return y[:, :d] + y[:, d:2 * d] + y[:, 2 * d:]


def _hgrn_state0_body(hf_ref, hi_ref, llb_ref, l1m_ref, s_ref):
    logf, k = _hgrn_gates(hf_ref[...], llb_ref[...], l1m_ref[...])
    n = logf.shape[0]
    b = _cumsum_rows(logf, _lower_tri(n))
    kd = k * jnp.exp(b[n - 1:n, :] - b)
    s_ref[0] = _dot(hi_ref[...].astype(F32).T.astype(BF16), kd.astype(BF16))


def _hgrn_state0(hf_m, hi_m, llb, l1m, nh):
    n, w = hf_m.shape
    d = w // nh
    col = lambda h: (0, h)
    return pl.pallas_call(
        _hgrn_state0_body,
        grid=(nh,),
        in_specs=[pl.BlockSpec((n, d), col), pl.BlockSpec((n, d), col),
                  pl.BlockSpec((1, d), col), pl.BlockSpec((1, d), col)],
        out_specs=pl.BlockSpec((1, d, d), lambda h: (h, 0, 0)),
        out_shape=jax.ShapeDtypeStruct((nh, d, d), F32),
        compiler_params=_params(("parallel",)),
        name="hgrn_state0",
    )(hf_m, hi_m, llb, l1m)


def _hgrn_body(hf_ref, hq_ref, hi_ref, hg_ref, llb_ref, l1m_ref, ng_ref, s0_ref, o_ref,
               st_sc, oc_sc, *, chunk, sub):
    r, d = hf_ref.shape

    @pl.when(pl.program_id(2) == 0)
    def _():
        st_sc[...] = s0_ref[0]

    logf, k = _hgrn_gates(hf_ref[...], llb_ref[...], l1m_ref[...])
    hq = hq_ref[...].astype(F32)
    q = hq * _sigmoid(hq)
    v = hi_ref[...].astype(F32)
    tri = _lower_tri(chunk)
    nchunk = r // chunk
    b = jnp.concatenate([_cumsum_rows(logf[c * chunk:(c + 1) * chunk], tri) for c in range(nchunk)],
                        axis=0)

    ones = jnp.ones((d, d), BF16)
    q3, k3, b3, v3 = [a.reshape(r // sub, sub, d) for a in (q, k, b, v)]
    row_in_sub = lax.broadcasted_iota(I32, (1, sub, 1), 1)
    o_diag = _dot((q * k).astype(BF16), ones) * v
    for off in range(1, sub):
        kr, br, vr = [pltpu.roll(a, off, 1) for a in (k3, b3, v3)]
        decay = jnp.exp(jnp.where(row_in_sub >= off, b3 - br, NEG_INF))
        prod = (q3 * kr * decay).reshape(r, d).astype(BF16)
        o_diag = o_diag + _dot(prod, ones) * vr.reshape(r, d)

    rowi = lax.broadcasted_iota(I32, (chunk, 1), 0)
    ri = lax.broadcasted_iota(I32, (chunk, chunk), 0)
    ci = lax.broadcasted_iota(I32, (chunk, chunk), 1)
    halves = []
    m = chunk // 2
    while m >= sub:
        sh = int(math.log2(2 * m))
        same_group = lax.shift_right_logical(ri, sh) == lax.shift_right_logical(ci, sh)
        halves.append((m, (rowi & (2 * m - 1)) >= m, same_group))
        m //= 2

    vb = v.astype(BF16)
    for c in range(nchunk):
        cs = slice(c * chunk, (c + 1) * chunk)
        qc, kc, bc, vc = q[cs], k[cs], b[cs], vb[cs]
        st = st_sc[...]
        blast = bc[chunk - 1:chunk, :]
        attn = jnp.zeros((chunk, chunk), F32)
        for m, upper, same_group in halves:
            anc = jnp.concatenate([jnp.broadcast_to(bc[g0 + m - 1:g0 + m, :], (2 * m, d))
                                   for g0 in range(0, chunk, 2 * m)], axis=0)
            q_m = (qc * jnp.exp(jnp.where(upper, bc - anc, NEG_INF))).astype(BF16)
            k_m = (kc * jnp.exp(jnp.where(upper, NEG_INF, anc - bc))).astype(BF16)
            attn = attn + jnp.where(same_group, _nt_dot(q_m, k_m), 0.0)
        oc_sc[cs, :] = (_nt_dot((qc * jnp.exp(bc)).astype(BF16), st.astype(BF16))
                        + _dot(attn.astype(BF16), vc))
        kd = (kc * jnp.exp(blast - bc)).astype(BF16)
        st_sc[...] = jnp.exp(blast) * st + _dot(vc.astype(F32).T.astype(BF16), kd)

    o = oc_sc[...] + o_diag
    o = _rms_norm(o, ng_ref[...])
    hg = hg_ref[...].astype(F32)
    o_ref[...] = (o * (hg * _sigmoid(hg))).astype(o_ref.dtype)


def _hgrn(hf, qig, llb, l1m, norm_g, s0t, batch, seq, rows):
    m, w = hf.shape
    nh = s0t.shape[0]
    d = w // nh
    nblk = seq // rows
    assert seq % rows == 0 and rows % HG_CHUNK == 0 and m == batch * seq
    body = functools.partial(_hgrn_body, chunk=HG_CHUNK, sub=HG_SUB)
    blk = lambda g: pl.BlockSpec((rows, d), lambda b, h, c, g=g: (b * nblk + c, g * nh + h))
    vec = pl.BlockSpec((1, d), lambda b, h, c: (0, h))
    return pl.pallas_call(
        body,
        grid=(batch, nh, nblk),
        in_specs=[blk(0), blk(0), blk(1), blk(2), vec, vec, vec,
                  pl.BlockSpec((1, d, d), lambda b, h, c: (h, 0, 0))],
        out_specs=blk(0),
        out_shape=jax.ShapeDtypeStruct((m, w), BF16),
        scratch_shapes=[pltpu.VMEM((d, d), F32), pltpu.VMEM((rows, d), F32)],
        compiler_params=_params(("parallel", "parallel", "arbitrary")),
        name="hgrn",
    )(hf, qig, qig, qig, llb, l1m, norm_g, s0t)


def _ln1_body(h_ref, mix_ref, g_ref, b_ref, wr_ref, br_ref, o32_ref, o16_ref, lg_ref):
    y = _layer_norm(DN_ALPHA * h_ref[...] + mix_ref[...], g_ref[...], b_ref[...])
    o32_ref[...] = y
    o16_ref[...] = y.astype(BF16)
    lg_ref[...] = jnp.dot(y, wr_ref[...], precision=lax.Precision.HIGHEST,
                          preferred_element_type=F32) + br_ref[...]


def _ln1_router(h, mix, g, b, w_router, b_router, tm):
    m, d = h.shape
    ne = w_router.shape[1]
    row = pl.BlockSpec((tm, d), lambda i: (i, 0))
    vec = pl.BlockSpec((1, d), lambda i: (0, 0))
    return pl.pallas_call(
        _ln1_body,
        grid=(pl.cdiv(m, tm),),
        in_specs=[row, row, vec, vec,
                  pl.BlockSpec((d, ne), lambda i: (0, 0)), pl.BlockSpec((1, ne), lambda i: (0, 0))],
        out_specs=[row, row, pl.BlockSpec((tm, ne), lambda i: (i, 0))],
        out_shape=[jax.ShapeDtypeStruct((m, d), F32), jax.ShapeDtypeStruct((m, d), BF16),
                   jax.ShapeDtypeStruct((m, ne), F32)],
        compiler_params=_params(("parallel",)),
        name="ln1_router",
    )(h, mix, g.reshape(1, d), b.reshape(1, d), w_router, b_router.reshape(1, ne))


def _expert_changed(be_ref):
    m = pl.program_id(1)
    return (m == 0) | (be_ref[m] != be_ref[jnp.maximum(m - 1, 0)])


def _moe_up_body(be_ref, nb_ref, x_ref, wg_ref, bg_ref, wu_ref, bu_ref, o_ref, wg_sc, wu_sc):
    @pl.when(_expert_changed(be_ref))
    def _():
        wg_sc[...] = wg_ref[0].astype(BF16)
        wu_sc[...] = wu_ref[0].astype(BF16)

    @pl.when(pl.program_id(1) < nb_ref[0])
    def _():
        x = x_ref[...]
        a = jnp.minimum(_dot(x, wg_sc[...]) + bg_ref[0], SWIGLU_LIMIT)
        u = jnp.clip(_dot(x, wu_sc[...]) + bu_ref[0], -SWIGLU_LIMIT, SWIGLU_LIMIT)
        o_ref[...] = ((u + 1.0) * a * _sigmoid(SWIGLU_ALPHA * a)).astype(o_ref.dtype)

    @pl.when(pl.program_id(1) >= nb_ref[0])
    def _():
        o_ref[...] = jnp.zeros(o_ref.shape, o_ref.dtype)


def _moe_down_body(be_ref, nb_ref, a_ref, wd_ref, bd_ref, g_ref, o_ref, wd_sc):
    @pl.when(_expert_changed(be_ref))
    def _():
        wd_sc[...] = wd_ref[0].astype(BF16)

    @pl.when(pl.program_id(1) < nb_ref[0])
    def _():
        y = _dot(a_ref[...], wd_sc[...]) + bd_ref[0]
        o_ref[...] = (y * g_ref[...]).astype(o_ref.dtype)

    @pl.when(pl.program_id(1) >= nb_ref[0])
    def _():
        o_ref[...] = jnp.zeros(o_ref.shape, o_ref.dtype)


def _moe_experts(xs, slot_gate, blk_expert, n_used, w_gate, b_gate, w_up, b_up, w_down, b_down,
                 tn_up, tn_down):
    cap, d = xs.shape
    ne, _, f = w_gate.shape
    nblk = cap // MOE_ROWS
    tn_up = min(tn_up, f)
    tn_down = min(tn_down, d)
    act = pl.pallas_call(
        _moe_up_body,
        grid_spec=pltpu.PrefetchScalarGridSpec(
            num_scalar_prefetch=2,
            grid=(f // tn_up, nblk),
            in_specs=[pl.BlockSpec((MOE_ROWS, d), lambda j, m, be, nb: (m, 0)),
                      pl.BlockSpec((1, d, tn_up), lambda j, m, be, nb: (be[m], 0, j)),
                      pl.BlockSpec((1, 1, tn_up), lambda j, m, be, nb: (be[m], 0, j)),
                      pl.BlockSpec((1, d, tn_up), lambda j, m, be, nb: (be[m], 0, j)),
                      pl.BlockSpec((1, 1, tn_up), lambda j, m, be, nb: (be[m], 0, j))],
            out_specs=pl.BlockSpec((MOE_ROWS, tn_up), lambda j, m, be, nb: (m, j)),
            scratch_shapes=[pltpu.VMEM((d, tn_up), BF16), pltpu.VMEM((d, tn_up), BF16)]),
        out_shape=jax.ShapeDtypeStruct((cap, f), BF16),
        compiler_params=_params(("arbitrary", "arbitrary")),
        name="moe_up",
    )(blk_expert, n_used, xs, w_gate, b_gate.reshape(ne, 1, f), w_up, b_up.reshape(ne, 1, f))
    return pl.pallas_call(
        _moe_down_body,
        grid_spec=pltpu.PrefetchScalarGridSpec(
            num_scalar_prefetch=2,
            grid=(d // tn_down, nblk),
            in_specs=[pl.BlockSpec((MOE_ROWS, f), lambda j, m, be, nb: (m, 0)),
                      pl.BlockSpec((1, f, tn_down), lambda j, m, be, nb: (be[m], 0, j)),
                      pl.BlockSpec((1, 1, tn_down), lambda j, m, be, nb: (be[m], 0, j)),
                      pl.BlockSpec((MOE_ROWS, 1), lambda j, m, be, nb: (m, 0))],
            out_specs=pl.BlockSpec((MOE_ROWS, tn_down), lambda j, m, be, nb: (m, j)),
            scratch_shapes=[pltpu.VMEM((f, tn_down), BF16)]),
        out_shape=jax.ShapeDtypeStruct((cap, d), BF16),
        compiler_params=_params(("arbitrary", "arbitrary")),
        name="moe_down",
    )(blk_expert, n_used, act, w_down, b_down.reshape(ne, 1, d), slot_gate.reshape(cap, 1))


def _moe_route(logits):
    n, ne = logits.shape
    nk = n * TOP_K
    top_val, top_idx = lax.top_k(logits, TOP_K)
    gate = jax.nn.softmax(top_val, axis=-1).reshape(nk)
    flat_e = top_idx.reshape(nk).astype(I32)
    onehot = (flat_e[:, None] == jnp.arange(ne, dtype=I32)[None, :]).astype(I32)
    csum = jnp.cumsum(onehot, axis=0)
    rank = jnp.sum(csum * onehot, axis=1) - 1
    counts = csum[-1]
    padded = (counts + MOE_ROWS - 1) // MOE_ROWS * MOE_ROWS
    pad_end = jnp.cumsum(padded)
    pos = (pad_end - padded)[flat_e] + rank
    nblk = -(-(nk + ne * (MOE_ROWS - 1)) // MOE_ROWS)
    cap = nblk * MOE_ROWS
    slot_pair = jnp.full((cap,), -1, I32).at[pos].set(jnp.arange(nk, dtype=I32), unique_indices=True)
    filled = slot_pair >= 0
    slot_pair = jnp.where(filled, slot_pair, jnp.arange(cap, dtype=I32) % nk)
    slot_tok = slot_pair // TOP_K
    slot_gate = jnp.where(filled, gate[slot_pair], 0.0)
    blk_expert = jnp.minimum(
        jnp.searchsorted(pad_end, jnp.arange(nblk, dtype=I32) * MOE_ROWS, side="right"),
        ne - 1).astype(I32)
    n_used = (pad_end[-1:] // MOE_ROWS).astype(I32)
    return slot_tok, slot_gate, pos.reshape(n, TOP_K), blk_expert, n_used


def _ln2_body(h_ref, y_ref, g_ref, b_ref, o_ref):
    ffn = y_ref[0].astype(F32)
    for c in range(1, y_ref.shape[0]):
        ffn = ffn + y_ref[c].astype(F32)
    o_ref[...] = _layer_norm(DN_ALPHA * h_ref[...] + ffn, g_ref[...], b_ref[...])


def _ln2(h, y4, g, b, tm):
    m, d = h.shape
    row = pl.BlockSpec((tm, d), lambda i: (i, 0))
    vec = pl.BlockSpec((1, d), lambda i: (0, 0))
    return pl.pallas_call(
        _ln2_body,
        grid=(pl.cdiv(m, tm),),
        in_specs=[row, pl.BlockSpec((y4.shape[0], tm, d), lambda i: (0, i, 0)), vec, vec],
        out_specs=row,
        out_shape=jax.ShapeDtypeStruct((m, d), F32),
        compiler_params=_params(("parallel",)),
        name="ln2",
    )(h, y4, g.reshape(1, d), b.reshape(1, d))


def _split_w_in(w_in):
    n_dsa = DSA_Q_RANK + DSA_KV_RANK + IDX_DIM + IDX_HEADS
    w_dsa = w_in[:, :n_dsa]
    w_hq, w_hf, w_hi, w_hg = [w_in[:, n_dsa + i * HG_WIDTH:n_dsa + (i + 1) * HG_WIDTH] for i in range(4)]
    w_qig = jnp.concatenate([w_hq, w_hi, w_hg], axis=1)
    return w_dsa.astype(BF16), w_hf.astype(BF16), w_qig.astype(BF16)


def _forward(x, meta_tokens, emb_ln_g, emb_ln_b, lb_logits, w_in, q_norm_g, kv_norm_g, w_uq, w_uk, w_uv,
             w_iq, idx_k_ln_g, idx_k_ln_b, hgrn_norm_g, w_o, ln1_g, ln1_b, w_router, b_router,
             w_gate, b_gate, w_up, b_up, w_down, b_down, ln2_g, ln2_b):
    batch, seq, d = x.shape
    n = batch * seq
    k_sel = min(IDX_TOPK_MAX, seq // 4)

    lb = jnp.cumsum(jax.nn.softmax(lb_logits.astype(F32), axis=0), axis=0)[0]
    llb = jnp.log(lb).reshape(1, HG_WIDTH)
    l1m = jnp.log1p(-lb).reshape(1, HG_WIDTH)
    w_dsa, w_hf, w_qig = _split_w_in(w_in)
    w_uqb = w_uq.astype(BF16)
    w_iqb = w_iq.astype(BF16)
    w_ukt3 = w_uk.transpose(1, 2, 0).astype(BF16)
    w_uv3 = w_uv.transpose(1, 0, 2).astype(BF16)

    def mixer_inputs(tokens, tm):
        h32, h16 = _ln_rows(tokens, emb_ln_g, emb_ln_b, 256)
        dsa_in = _matmul(h16, w_dsa, F32, 512, w_dsa.shape[1], "proj_dsa")
        hf = _matmul(h16, w_hf, F32, tm, 512, "proj_hf")
        qig = _matmul(h16, w_qig, BF16, tm, 512, "proj_qig")
        prep = _dsa_prep(dsa_in, q_norm_g, kv_norm_g, idx_k_ln_g, idx_k_ln_b, w_uqb, w_ukt3, w_iqb, 256)
        return h32, hf, qig, prep

    _, hf_m, qig_m, (_, _, ckv_m, _, _) = mixer_inputs(meta_tokens.astype(F32), N_META)
    s0t = _hgrn_state0(hf_m, qig_m[:, HG_WIDTH:2 * HG_WIDTH], llb, l1m, HG_HEADS)

    h0, hf, qig, (qa, iq, ckv, ikn, iwt) = mixer_inputs(x.reshape(n, d), 1024)
    a = _dsa_attention(qa, iq, iwt, ikn, ckv, ckv_m, w_uv3, batch, seq, k_sel, 128, 512)
    g = _hgrn(hf, qig, llb, l1m, hgrn_norm_g.reshape(1, HG_WIDTH), s0t, batch, seq, 1024)

    mix = _matmul(jnp.concatenate([a, g], axis=1), w_o.astype(BF16), F32, 1024, 512, "proj_out")
    h1, h1b, logits = _ln1_router(h0, mix, ln1_g, ln1_b, w_router, b_router, 256)

    slot_tok, slot_gate, pos, blk_expert, n_used = _moe_route(logits)
    xs = jnp.take(h1b, slot_tok, axis=0, mode="clip")
    y = _moe_experts(xs, slot_gate, blk_expert, n_used, w_gate, b_gate, w_up, b_up, w_down, b_down, 256, 2048)
    y4 = jnp.take(y, pos.T.reshape(TOP_K * n), axis=0, mode="clip").reshape(TOP_K, n, d)
    out = _ln2(h1, y4, ln2_g, ln2_b, 128)
    return out.reshape(batch, seq, d)


def kernel(x, meta_tokens, emb_ln_g, emb_ln_b, lb_logits, w_in, q_norm_g, kv_norm_g, w_uq, w_uk, w_uv, w_iq, idx_k_ln_g, idx_k_ln_b, hgrn_norm_g, w_o, ln1_g, ln1_b, w_router, b_router, w_gate, b_gate, w_up, b_up, w_down, b_down, ln2_g, ln2_b):
    return _forward(x, meta_tokens, emb_ln_g, emb_ln_b, lb_logits, w_in[0], q_norm_g[0], kv_norm_g[0],
                    w_uq[0], w_uk[0], w_uv[0], w_iq[0], idx_k_ln_g[0], idx_k_ln_b[0], hgrn_norm_g[0],
                    w_o[0], ln1_g[0], ln1_b[0], w_router[0], b_router[0], w_gate[0], b_gate[0],
                    w_up[0], b_up[0], w_down[0], b_down[0], ln2_g[0], ln2_b[0])
```

```python
import functools
import math

import jax
import jax.numpy as jnp
from jax import lax
from jax.experimental import pallas as pl
from jax.experimental.pallas import tpu as pltpu

F32 = jnp.float32
BF16 = jnp.bfloat16
I32 = jnp.int32

N_META = 16
DSA_HEADS = 16
DSA_HEAD_DIM = 128
DSA_Q_RANK = 768
DSA_KV_RANK = 512
IDX_HEADS = 16
IDX_DIM = 64
IDX_TOPK_MAX = 256
HG_HEADS = 16
HG_DK = 128
HG_DV = 128
HG_WIDTH = HG_HEADS * HG_DK
HG_CHUNK = 64
HG_SUB = 8
N_EXPERTS = 32
TOP_K = 4
SWIGLU_LIMIT = 7.0
SWIGLU_ALPHA = 1.702
LN_EPS = 1e-5
RMS_EPS = 1e-6
DEPTH = 1
DN_ALPHA = (2 * DEPTH) ** 0.25

V7X_VMEM_LIMIT_BYTES = 56 * 1024 * 1024
MOE_ROWS = 256
INT_MIN = -(2 ** 31)
NEG_INF = float("-inf")
LOG2E = 1.4426950408889634


def _params(sem):
    return pltpu.CompilerParams(dimension_semantics=sem, vmem_limit_bytes=V7X_VMEM_LIMIT_BYTES)


def _nt_dot(a, b):
    return lax.dot_general(a, b, (((1,), (1,)), ((), ())), preferred_element_type=F32)


def _dot(a, b):
    return jnp.dot(a, b, preferred_element_type=F32)


def _layer_norm(x, g, b):
    mu = jnp.mean(x, axis=-1, keepdims=True)
    xc = x - mu
    var = jnp.mean(xc * xc, axis=-1, keepdims=True)
    return xc * lax.rsqrt(var + LN_EPS) * g + b


def _rms_norm(x, g):
    ms = jnp.mean(x * x, axis=-1, keepdims=True)
    return x * lax.rsqrt(ms + RMS_EPS) * g


def _sigmoid(x):
    return 0.5 * jnp.tanh(0.5 * x) + 0.5


def _ln_body(x_ref, g_ref, b_ref, o32_ref, o16_ref):
    y = _layer_norm(x_ref[...].astype(F32), g_ref[...], b_ref[...])
    o32_ref[...] = y
    o16_ref[...] = y.astype(BF16)


def _ln_rows(x, g, b, tm):
    m, d = x.shape
    tm = min(tm, m)
    row = pl.BlockSpec((tm, d), lambda i: (i, 0))
    vec = pl.BlockSpec((1, d), lambda i: (0, 0))
    return pl.pallas_call(
        _ln_body,
        grid=(pl.cdiv(m, tm),),
        in_specs=[row, vec, vec],
        out_specs=[row, row],
        out_shape=[jax.ShapeDtypeStruct((m, d), F32), jax.ShapeDtypeStruct((m, d), BF16)],
        compiler_params=_params(("parallel",)),
        name="embed_ln",
    )(x, g.reshape(1, d), b.reshape(1, d))


def _mm_body(a_ref, b_ref, o_ref):
    o_ref[...] = _dot(a_ref[...], b_ref[...]).astype(o_ref.dtype)


def _matmul(a, b, out_dtype, tm, tn, name):
    m, k = a.shape
    n = b.shape[1]
    tm = min(tm, m)
    tn = min(tn, n)
    return pl.pallas_call(
        _mm_body,
        grid=(pl.cdiv(m, tm), pl.cdiv(n, tn)),
        in_specs=[pl.BlockSpec((tm, k), lambda i, j: (i, 0)),
                  pl.BlockSpec((k, tn), lambda i, j: (0, j))],
        out_specs=pl.BlockSpec((tm, tn), lambda i, j: (i, j)),
        out_shape=jax.ShapeDtypeStruct((m, n), out_dtype),
        compiler_params=_params(("parallel", "arbitrary")),
        name=name,
    )(a, b)


def _mm_pair_body(a1_ref, a2_ref, b1_ref, b2_ref, o_ref):
    o_ref[...] = (_dot(a1_ref[...], b1_ref[...]) + _dot(a2_ref[...], b2_ref[...])).astype(o_ref.dtype)


def _matmul_pair(a1, a2, b, out_dtype, tm, tn, name):
    m, k = a1.shape
    n = b.shape[1]
    assert a2.shape == (m, k) and b.shape[0] == 2 * k
    return pl.pallas_call(
        _mm_pair_body,
        grid=(pl.cdiv(m, tm), pl.cdiv(n, tn)),
        in_specs=[pl.BlockSpec((tm, k), lambda i, j: (i, 0)),
                  pl.BlockSpec((tm, k), lambda i, j: (i, 0)),
                  pl.BlockSpec((k, tn), lambda i, j: (0, j)),
                  pl.BlockSpec((k, tn), lambda i, j: (1, j))],
        out_specs=pl.BlockSpec((tm, tn), lambda i, j: (i, j)),
        out_shape=jax.ShapeDtypeStruct((m, n), out_dtype),
        compiler_params=_params(("parallel", "arbitrary")),
        name=name,
    )(a1, a2, b, b)


def _dsa_prep_body(x_ref, qg_ref, kvg_ref, ikg_ref, ikb_ref, wuq_ref, wukt_ref, wiq_ref,
                   qa_ref, iq_ref, ckv_ref, ikn_ref, iwt_ref, *, iw_scale, qk_scale):
    rq = qg_ref.shape[1]
    rkv = kvg_ref.shape[1]
    di = ikg_ref.shape[1]
    nh, dh, _ = wukt_ref.shape
    x = x_ref[...]
    cq = _rms_norm(x[:, :rq], qg_ref[...]).astype(BF16)
    ckv_ref[...] = _rms_norm(x[:, rq:rq + rkv], kvg_ref[...]).astype(BF16)
    ik = x[:, rq + rkv:rq + rkv + di]
    ikn_ref[...] = _layer_norm(ik, ikg_ref[...], ikb_ref[...]).astype(BF16)
    iwt_ref[...] = x[:, rq + rkv + di:] * iw_scale
    q = _dot(cq, wuq_ref[...]).astype(BF16)
    iq = _dot(cq, wiq_ref[...]).astype(BF16)
    for h in range(nh):
        qa_ref[h] = (_dot(q[:, h * dh:(h + 1) * dh], wukt_ref[h]) * qk_scale).astype(BF16)
        iq_ref[h] = iq[:, h * di:(h + 1) * di]


def _dsa_prep(x, q_g, kv_g, ik_g, ik_b, w_uq, w_ukt3, w_iq, tm):
    m, w = x.shape
    nh, dh, rkv = w_ukt3.shape
    rq = w_uq.shape[0]
    di = w_iq.shape[1] // nh
    nih = w - rq - rkv - di
    assert nih == nh and w_uq.shape[1] == nh * dh
    tm = min(tm, m)
    vec = lambda n: pl.BlockSpec((1, n), lambda i: (0, 0))
    full = lambda a: pl.BlockSpec(a.shape, lambda i: (0,) * a.ndim)
    return pl.pallas_call(
        functools.partial(_dsa_prep_body, iw_scale=nih ** -0.5 * di ** -0.5, qk_scale=dh ** -0.5 * LOG2E),
        grid=(pl.cdiv(m, tm),),
        in_specs=[pl.BlockSpec((tm, w), lambda i: (i, 0)),
                  vec(rq), vec(rkv), vec(di), vec(di), full(w_uq), full(w_ukt3), full(w_iq)],
        out_specs=[pl.BlockSpec((nh, tm, rkv), lambda i: (0, i, 0)),
                   pl.BlockSpec((nh, tm, di), lambda i: (0, i, 0)),
                   pl.BlockSpec((tm, rkv), lambda i: (i, 0)),
                   pl.BlockSpec((tm, di), lambda i: (i, 0)),
                   pl.BlockSpec((tm, nih), lambda i: (i, 0))],
        out_shape=[jax.ShapeDtypeStruct((nh, m, rkv), BF16),
                   jax.ShapeDtypeStruct((nh, m, di), BF16),
                   jax.ShapeDtypeStruct((m, rkv), BF16),
                   jax.ShapeDtypeStruct((m, di), BF16),
                   jax.ShapeDtypeStruct((m, nih), F32)],
        compiler_params=_params(("parallel",)),
        name="dsa_prep",
    )(x, q_g.reshape(1, rq), kv_g.reshape(1, rkv), ik_g.reshape(1, di), ik_b.reshape(1, di),
      w_uq, w_ukt3, w_iq)


def _dsa_attn_body(qa_ref, iq_ref, iwt_ref, ik_ref, ckv_ref, ckvm_ref, wuv_ref, o_ref,
                   key_sc, m_sc, l_sc, al_sc, acc_sc, p_sc, sa_sc, sb_sc, *, k_sel, n_meta, slopes):
    nh, tq, rkv = qa_ref.shape
    di = iq_ref.shape[2]
    nkb_all, _, kblk = key_sc.shape
    mpad = ckvm_ref.shape[0]
    dh = wuv_ref.shape[2]
    rows = nh * tq
    t0 = pl.program_id(1) * tq
    t_col = t0 + lax.broadcasted_iota(I32, (tq, 1), 0)
    hs = [slice(h * tq, (h + 1) * tq) for h in range(nh)]
    nkb = (t0 + tq - 1) // kblk + 1
    c_slope = [s * LOG2E for s in slopes]
    heads_per_group = 4
    slab = tq

    iq2 = iq_ref[...].reshape(rows, di)
    iw = iwt_ref[...]
    for cb in range(nkb_all):
        @pl.when(cb < nkb)
        def _():
            sc = _nt_dot(iq2, ik_ref[cb * kblk:(cb + 1) * kblk, :])
            tot = jnp.zeros((tq, kblk), F32)
            for h in range(nh):
                tot = tot + iw[:, h:h + 1] * jnp.maximum(sc[hs[h]], 0.0)
            bits = lax.bitcast_convert_type(tot, I32)
            key = jnp.where(bits < 0, bits ^ jnp.int32(0x7FFFFFFF), bits)
            col = cb * kblk + lax.broadcasted_iota(I32, (1, kblk), 1)
            key_sc[cb] = jnp.where(col <= t_col, key, jnp.int32(INT_MIN))

    lanes = 128
    half = tq // 2
    halves = (slice(0, half), slice(half, tq))

    def fold(combine, elem, init, nblk, rs):
        a = jnp.full((half, lanes), init)
        for cb in range(nblk):
            e = elem(key_sc[cb, rs, :])
            for c in range(kblk // lanes):
                a = combine(a, e[:, c * lanes:(c + 1) * lanes])
        return a

    def threshold(nblk):
        def count_ge(cand, rs):
            part = fold(jnp.add, lambda kc: (kc >= cand).astype(F32), jnp.float32(0.0), nblk, rs)
            return jnp.sum(part, axis=1, keepdims=True)

        def probe(st, mid, rs):
            lo, hi, open_f = st
            is_open = open_f > 0.0
            cnt = count_ge(mid, rs)
            ge = cnt >= float(k_sel)
            lo = jnp.where(is_open & ge, mid, lo)
            hi = jnp.where(is_open & ge & (cnt == float(k_sel)), mid, jnp.where(is_open & ~ge, mid - 1, hi))
            return lo, hi, (is_open & (lo < hi)).astype(F32)

        def midpoint(st):
            return (st[0] | st[1]) - lax.shift_right_arithmetic(st[0] ^ st[1], 1)

        states = []
        for rs in halves:
            kmax = jnp.max(fold(jnp.maximum, lambda kc: kc, jnp.int32(INT_MIN), nblk, rs), axis=1, keepdims=True)
            lo0 = jnp.full((half, 1), INT_MIN + 1, I32)
            open0 = (t_col[rs] + 1 > k_sel) & (kmax > lo0)
            first = jnp.where(kmax > INT_MIN + 1 + (1 << 24), kmax - (1 << 24), lo0 + 1)
            states.append(probe((lo0, kmax, open0.astype(F32)), first, rs))

        def n_open(sts):
            return jnp.sum(sts[0][2]) + jnp.sum(sts[1][2])

        def trip(carry):
            it, sts, _ = carry
            sts = list(sts)
            for _ in range(4):
                for i, rs in enumerate(halves):
                    sts[i] = probe(sts[i], midpoint(sts[i]), rs)
            return it + 1, tuple(sts), n_open(sts)

        out = lax.while_loop(lambda c: (c[0] < 12) & (c[2] > 0.0), trip,
                             (jnp.int32(0), tuple(states), n_open(states)))
        return jnp.concatenate([out[1][0][0], out[1][1][0]], axis=0)

    thr = lax.switch(nkb - 1, [functools.partial(threshold, c) for c in range(1, nkb_all + 1)])

    qa2 = qa_ref[...].reshape(rows, rkv)
    ckvm = ckvm_ref[...]
    sm = _nt_dot(qa2, ckvm)
    jm = lax.broadcasted_iota(I32, (1, mpad), 1)
    pos_m = (jm - n_meta - t0).astype(F32)
    for h in range(nh):
        s_h = jnp.where(jm < n_meta, sm[hs[h]] + c_slope[h] * pos_m, NEG_INF)
        m_h = jnp.max(s_h, axis=1, keepdims=True)
        p = jnp.exp2(s_h - m_h)
        m_sc[hs[h]] = m_h
        l_sc[hs[h]] = jnp.sum(p, axis=1, keepdims=True)
        p_sc[hs[h], :mpad] = p.astype(BF16)
    acc_sc[...] = _dot(p_sc[:, :mpad], ckvm)

    def kv_block(kb):
        return ckv_ref[pl.ds(pl.multiple_of(kb * kblk, kblk), kblk), :]

    def qk_into(dst, kb):
        dst[...] = _nt_dot(qa2, kv_block(jnp.minimum(kb, nkb - 1)))

    def softmax_pv(src, kb):
        pos = (kb * kblk - t0 + lax.broadcasted_iota(I32, (1, kblk), 1)).astype(F32)
        neg = jnp.where(key_sc[kb] >= thr, 0.0, NEG_INF)
        kv = kv_block(kb)
        m_all = m_sc[...]
        l_all = l_sc[...]
        for g in range(0, nh, heads_per_group):
            for h in range(g, g + heads_per_group):
                bias = neg + c_slope[h] * pos
                for r in range(0, tq, slab):
                    rs = slice(h * tq + r, h * tq + r + slab)
                    s_r = src[rs, :] + bias[r:r + slab]
                    m_old = m_all[rs]
                    m_new = jnp.maximum(m_old, jnp.max(s_r, axis=1, keepdims=True))
                    alpha = jnp.exp2(m_old - m_new)
                    p = jnp.exp2(s_r - m_new)
                    l_sc[rs] = alpha * l_all[rs] + jnp.sum(p, axis=1, keepdims=True)
                    m_sc[rs] = m_new
                    al_sc[rs] = alpha
                    p_sc[rs] = p.astype(BF16)
            gs = slice(g * tq, (g + heads_per_group) * tq)
            acc_sc[gs] = acc_sc[gs] * al_sc[gs] + _dot(p_sc[gs], kv)

    qk_into(sa_sc, 0)

    def two_blocks(i, carry):
        kb = 2 * i
        qk_into(sb_sc, kb + 1)
        softmax_pv(sa_sc, kb)

        @pl.when(kb + 1 < nkb)
        def _():
            qk_into(sa_sc, kb + 2)
            softmax_pv(sb_sc, kb + 1)
        return carry

    lax.fori_loop(0, (nkb + 1) // 2, two_blocks, 0)

    inv_l = 1.0 / l_sc[...]
    for h in range(nh):
        out_h = _dot(acc_sc[hs[h]].astype(BF16), wuv_ref[h]) * inv_l[hs[h]]
        o_ref[:, h * dh:(h + 1) * dh] = out_h.astype(o_ref.dtype)


def _dsa_attention(qa, iq, iwt, ikn, ckv, ckv_m, w_uv3, batch, seq, k_sel, tq, kblk):
    nh, m, rkv = qa.shape
    di = iq.shape[2]
    dh = w_uv3.shape[2]
    n_meta = ckv_m.shape[0]
    mpad = 128
    assert n_meta <= mpad <= kblk and seq % kblk == 0 and seq % tq == 0 and m == batch * seq
    ckv_mp = jnp.zeros((mpad, rkv), BF16).at[:n_meta].set(ckv_m)
    nq = seq // tq
    slopes = tuple(2.0 ** (-8.0 * (h + 1) / nh) for h in range(nh))
    body = functools.partial(_dsa_attn_body, k_sel=k_sel, n_meta=n_meta, slopes=slopes)
    return pl.pallas_call(
        body,
        grid=(batch, nq),
        in_specs=[pl.BlockSpec((nh, tq, rkv), lambda b, q: (0, b * nq + q, 0)),
                  pl.BlockSpec((nh, tq, di), lambda b, q: (0, b * nq + q, 0)),
                  pl.BlockSpec((tq, nh), lambda b, q: (b * nq + q, 0)),
                  pl.BlockSpec((seq, di), lambda b, q: (b, 0)),
                  pl.BlockSpec((seq, rkv), lambda b, q: (b, 0)),
                  pl.BlockSpec((mpad, rkv), lambda b, q: (0, 0)),
                  pl.BlockSpec((nh, rkv, dh), lambda b, q: (0, 0, 0))],
        out_specs=pl.BlockSpec((tq, nh * dh), lambda b, q: (b * nq + q, 0)),
        out_shape=jax.ShapeDtypeStruct((m, nh * dh), BF16),
        scratch_shapes=[pltpu.VMEM((seq // kblk, tq, kblk), I32),
                        pltpu.VMEM((nh * tq, 1), F32),
                        pltpu.VMEM((nh * tq, 1), F32),
                        pltpu.VMEM((nh * tq, 1), F32),
                        pltpu.VMEM((nh * tq, rkv), F32),
                        pltpu.VMEM((nh * tq, kblk), BF16),
                        pltpu.VMEM((nh * tq, kblk), F32),
                        pltpu.VMEM((nh * tq, kblk), F32)],
        compiler_params=_params(("parallel", "arbitrary")),
        name="dsa_attention",
    )(qa, iq, iwt, ikn, ckv, ckv_mp, w_uv3)


def _hgrn_gates(hf, llb, l1m):
    softplus = jnp.log(1.0 + jnp.exp(-jnp.abs(hf)))
    c = l1m + (jnp.minimum(hf, 0.0) - softplus)
    mx = jnp.maximum(llb, c)
    logf = mx + jnp.log(1.0 + jnp.exp(-jnp.abs(llb - c)))
    k = jnp.exp(l1m + (jnp.minimum(-hf, 0.0) - softplus))
    return logf, k


def _lower_tri(n):
    ri = lax.broadcasted_iota(I32, (n, n), 0)
    ci = lax.broadcasted_iota(I32, (n, n), 1)
    return (ci <= ri).astype(BF16)


def _cumsum_rows(x, tri):
    d = x.shape[1]
    t1 = x.astype(BF16)
    r1 = x - t1.astype(F32)
    t2 = r1.astype(BF16)
    t3 = (r1 - t2.astype(F32)).astype(BF16)
    y = _dot(tri, jnp.concatenate([t1, t2, t3], axis=1))
    return y[:, :d] + y[:, d:2 * d] + y[:, 2 * d:]


def _hgrn_state0_body(hf_ref, hi_ref, llb_ref, l1m_ref, s_ref):
    logf, k = _hgrn_gates(hf_ref[...], llb_ref[...], l1m_ref[...])
    n = logf.shape[0]
    b = _cumsum_rows(logf, _lower_tri(n))
    kd = k * jnp.exp(b[n - 1:n, :] - b)
    s_ref[0] = _dot(hi_ref[...].astype(F32).T.astype(BF16), kd.astype(BF16))


def _hgrn_state0(hf_m, hi_m, llb, l1m, nh):
    n, w = hf_m.shape
    d = w // nh
    col = lambda h: (0, h)
    return pl.pallas_call(
        _hgrn_state0_body,
        grid=(nh,),
        in_specs=[pl.BlockSpec((n, d), col), pl.BlockSpec((n, d), col),
                  pl.BlockSpec((1, d), col), pl.BlockSpec((1, d), col)],
        out_specs=pl.BlockSpec((1, d, d), lambda h: (h, 0, 0)),
        out_shape=jax.ShapeDtypeStruct((nh, d, d), F32),
        compiler_params=_params(("parallel",)),
        name="hgrn_state0",
    )(hf_m, hi_m, llb, l1m)


def _hgrn_body(hf_ref, hq_ref, hi_ref, hg_ref, llb_ref, l1m_ref, ng_ref, s0_ref, o_ref,
               st_sc, oc_sc, *, chunk, sub):
    r, d = hf_ref.shape

    @pl.when(pl.program_id(2) == 0)
    def _():
        st_sc[...] = s0_ref[0]

    logf, k = _hgrn_gates(hf_ref[...], llb_ref[...], l1m_ref[...])
    hq = hq_ref[...].astype(F32)
    q = hq * _sigmoid(hq)
    v = hi_ref[...].astype(F32)
    tri = _lower_tri(chunk)
    nchunk = r // chunk
    b = jnp.concatenate([_cumsum_rows(logf[c * chunk:(c + 1) * chunk], tri) for c in range(nchunk)],
                        axis=0)

    ones = jnp.ones((d, d), BF16)
    q3, k3, b3, v3 = [a.reshape(r // sub, sub, d) for a in (q, k, b, v)]
    row_in_sub = lax.broadcasted_iota(I32, (1, sub, 1), 1)
    o_diag = _dot((q * k).astype(BF16), ones) * v
    for off in range(1, sub):
        kr, br, vr = [pltpu.roll(a, off, 1) for a in (k3, b3, v3)]
        decay = jnp.exp(jnp.where(row_in_sub >= off, b3 - br, NEG_INF))
        prod = (q3 * kr * decay).reshape(r, d).astype(BF16)
        o_diag = o_diag + _dot(prod, ones) * vr.reshape(r, d)

    rowi = lax.broadcasted_iota(I32, (chunk, 1), 0)
    ri = lax.broadcasted_iota(I32, (chunk, chunk), 0)
    ci = lax.broadcasted_iota(I32, (chunk, chunk), 1)
    halves = []
    m = chunk // 2
    while m >= sub:
        sh = int(math.log2(2 * m))
        same_group = lax.shift_right_logical(ri, sh) == lax.shift_right_logical(ci, sh)
        halves.append((m, (rowi & (2 * m - 1)) >= m, same_group))
        m //= 2

    vb = v.astype(BF16)
    for c in range(nchunk):
        cs = slice(c * chunk, (c + 1) * chunk)
        qc, kc, bc, vc = q[cs], k[cs], b[cs], vb[cs]
        st = st_sc[...]
        blast = bc[chunk - 1:chunk, :]
        attn = jnp.zeros((chunk, chunk), F32)
        for m, upper, same_group in halves:
            anc = jnp.concatenate([jnp.broadcast_to(bc[g0 + m - 1:g0 + m, :], (2 * m, d))
                                   for g0 in range(0, chunk, 2 * m)], axis=0)
            q_m = (qc * jnp.exp(jnp.where(upper, bc - anc, NEG_INF))).astype(BF16)
            k_m = (kc * jnp.exp(jnp.where(upper, NEG_INF, anc - bc))).astype(BF16)
            attn = attn + jnp.where(same_group, _nt_dot(q_m, k_m), 0.0)
        oc_sc[cs, :] = (_nt_dot((qc * jnp.exp(bc)).astype(BF16), st.astype(BF16))
                        + _dot(attn.astype(BF16), vc))
        kd = (kc * jnp.exp(blast - bc)).astype(BF16)
        st_sc[...] = jnp.exp(blast) * st + _dot(vc.astype(F32).T.astype(BF16), kd)

    o = oc_sc[...] + o_diag
    o = _rms_norm(o, ng_ref[...])
    hg = hg_ref[...].astype(F32)
    o_ref[...] = (o * (hg * _sigmoid(hg))).astype(o_ref.dtype)


def _hgrn(hf, qig, llb, l1m, norm_g, s0t, batch, seq, rows):
    m, w = hf.shape
    nh = s0t.shape[0]
    d = w // nh
    rows = min(rows, seq)
    nblk = seq // rows
    assert seq % rows == 0 and rows % HG_CHUNK == 0 and m == batch * seq
    body = functools.partial(_hgrn_body, chunk=HG_CHUNK, sub=HG_SUB)
    blk = lambda g: pl.BlockSpec((rows, d), lambda b, h, c, g=g: (b * nblk + c, g * nh + h))
    vec = pl.BlockSpec((1, d), lambda b, h, c: (0, h))
    return pl.pallas_call(
        body,
        grid=(batch, nh, nblk),
        in_specs=[blk(0), blk(0), blk(1), blk(2), vec, vec, vec,
                  pl.BlockSpec((1, d, d), lambda b, h, c: (h, 0, 0))],
        out_specs=blk(0),
        out_shape=jax.ShapeDtypeStruct((m, w), BF16),
        scratch_shapes=[pltpu.VMEM((d, d), F32), pltpu.VMEM((rows, d), F32)],
        compiler_params=_params(("parallel", "parallel", "arbitrary")),
        name="hgrn",
    )(hf, qig, qig, qig, llb, l1m, norm_g, s0t)


def _ln1_body(h_ref, mix_ref, g_ref, b_ref, wr_ref, br_ref, o32_ref, o16_ref, lg_ref):
    y = _layer_norm(DN_ALPHA * h_ref[...] + mix_ref[...], g_ref[...], b_ref[...])
    o32_ref[...] = y
    o16_ref[...] = y.astype(BF16)
    lg_ref[...] = jnp.dot(y, wr_ref[...], precision=lax.Precision.HIGHEST,
                          preferred_element_type=F32) + br_ref[...]


def _ln1_router(h, mix, g, b, w_router, b_router, tm):
    m, d = h.shape
    ne = w_router.shape[1]
    row = pl.BlockSpec((tm, d), lambda i: (i, 0))
    vec = pl.BlockSpec((1, d), lambda i: (0, 0))
    return pl.pallas_call(
        _ln1_body,
        grid=(pl.cdiv(m, tm),),
        in_specs=[row, row, vec, vec,
                  pl.BlockSpec((d, ne), lambda i: (0, 0)), pl.BlockSpec((1, ne), lambda i: (0, 0))],
        out_specs=[row, row, pl.BlockSpec((tm, ne), lambda i: (i, 0))],
        out_shape=[jax.ShapeDtypeStruct((m, d), F32), jax.ShapeDtypeStruct((m, d), BF16),
                   jax.ShapeDtypeStruct((m, ne), F32)],
        compiler_params=_params(("parallel",)),
        name="ln1_router",
    )(h, mix, g.reshape(1, d), b.reshape(1, d), w_router, b_router.reshape(1, ne))


def _stream_expert_rows(start_ref, nblk_ref, x_hbm, o_hbm, x_buf, o_buf, x_sem, o_sem, setup, compute):
    e = pl.program_id(0)
    j = pl.program_id(1)
    rows, tn = o_buf.shape[1], o_buf.shape[2]
    base = start_ref[e]
    nb = nblk_ref[e]

    def x_copy(i, slot):
        r0 = pl.multiple_of(base + i * rows, rows)
        return pltpu.make_async_copy(x_hbm.at[pl.ds(r0, rows)], x_buf.at[slot], x_sem.at[slot])

    def o_copy(i, slot):
        r0 = pl.multiple_of(base + i * rows, rows)
        c0 = pl.multiple_of(j * tn, tn)
        return pltpu.make_async_copy(o_buf.at[slot], o_hbm.at[pl.ds(r0, rows), pl.ds(c0, tn)], o_sem.at[slot])

    @pl.when(nb > 0)
    def _():
        x_copy(0, 0).start()

    setup()

    def row_block(i, carry):
        slot = i & 1
        x_copy(i, slot).wait()

        @pl.when(i + 1 < nb)
        def _():
            x_copy(i + 1, 1 - slot).start()

        @pl.when(i >= 2)
        def _():
            o_copy(i - 2, slot).wait()

        o_buf[slot] = compute(x_buf[slot]).astype(o_buf.dtype)
        o_copy(i, slot).start()
        return carry

    lax.fori_loop(0, nb, row_block, 0)

    @pl.when(nb >= 2)
    def _():
        o_copy(nb - 2, nb & 1).wait()

    @pl.when(nb >= 1)
    def _():
        o_copy(nb - 1, (nb - 1) & 1).wait()

    n_exp = pl.num_programs(0)

    @pl.when(e == n_exp - 1)
    def _():
        o_buf[0] = jnp.zeros(o_buf.shape[1:], o_buf.dtype)

        def zero_block(t, carry):
            r0 = pl.multiple_of(start_ref[n_exp] + t * rows, rows)
            c0 = pl.multiple_of(j * tn, tn)
            cp = pltpu.make_async_copy(o_buf.at[0], o_hbm.at[pl.ds(r0, rows), pl.ds(c0, tn)], o_sem.at[0])
            cp.start()
            cp.wait()
            return carry

        lax.fori_loop(0, nblk_ref[n_exp], zero_block, 0)


def _moe_up_body(start_ref, nblk_ref, x_hbm, wg_ref, bg_ref, wu_ref, bu_ref, o_hbm,
                 wg_sc, wu_sc, x_buf, o_buf, x_sem, o_sem):
    def cast_weights():
        wg_sc[...] = wg_ref[0].astype(BF16)
        wu_sc[...] = wu_ref[0].astype(BF16)

    def swiglu(x):
        a = jnp.minimum(_dot(x, wg_sc[...]) + bg_ref[0], SWIGLU_LIMIT)
        u = jnp.clip(_dot(x, wu_sc[...]) + bu_ref[0], -SWIGLU_LIMIT, SWIGLU_LIMIT)
        return (u + 1.0) * a * _sigmoid(SWIGLU_ALPHA * a)

    _stream_expert_rows(start_ref, nblk_ref, x_hbm, o_hbm, x_buf, o_buf, x_sem, o_sem, cast_weights, swiglu)


def _moe_down_body(start_ref, nblk_ref, a_hbm, wd_ref, bd_ref, o_hbm, wd_sc, a_buf, o_buf, a_sem, o_sem):
    def cast_weights():
        wd_sc[...] = wd_ref[0].astype(BF16)

    def down(a):
        return _dot(a, wd_sc[...]) + bd_ref[0]

    _stream_expert_rows(start_ref, nblk_ref, a_hbm, o_hbm, a_buf, o_buf, a_sem, o_sem, cast_weights, down)


def _moe_experts(xs, exp_start, exp_nblk, w_gate, b_gate, w_up, b_up, w_down, b_down, tn_up, tn_down):
    cap, d = xs.shape
    ne, _, f = w_gate.shape
    tn_up = min(tn_up, f)
    tn_down = min(tn_down, d)
    any_space = pl.BlockSpec(memory_space=pl.ANY)
    stream_scratch = lambda k, tn: [pltpu.VMEM((2, MOE_ROWS, k), BF16), pltpu.VMEM((2, MOE_ROWS, tn), BF16),
                                    pltpu.SemaphoreType.DMA((2,)), pltpu.SemaphoreType.DMA((2,))]
    w_tile = lambda k, tn: pl.BlockSpec((1, k, tn), lambda e, j, st, nb: (e, 0, j))
    act = pl.pallas_call(
        _moe_up_body,
        grid_spec=pltpu.PrefetchScalarGridSpec(
            num_scalar_prefetch=2,
            grid=(ne, f // tn_up),
            in_specs=[any_space, w_tile(d, tn_up), w_tile(1, tn_up), w_tile(d, tn_up), w_tile(1, tn_up)],
            out_specs=any_space,
            scratch_shapes=[pltpu.VMEM((d, tn_up), BF16), pltpu.VMEM((d, tn_up), BF16)]
            + stream_scratch(d, tn_up)),
        out_shape=jax.ShapeDtypeStruct((cap, f), BF16),
        compiler_params=_params(("arbitrary", "arbitrary")),
        name="moe_up",
    )(exp_start, exp_nblk, xs, w_gate, b_gate.reshape(ne, 1, f), w_up, b_up.reshape(ne, 1, f))
    return pl.pallas_call(
        _moe_down_body,
        grid_spec=pltpu.PrefetchScalarGridSpec(
            num_scalar_prefetch=2,
            grid=(ne, d // tn_down),
            in_specs=[any_space, w_tile(f, tn_down), w_tile(1, tn_down)],
            out_specs=any_space,
            scratch_shapes=[pltpu.VMEM((f, tn_down), BF16)] + stream_scratch(f, tn_down)),
        out_shape=jax.ShapeDtypeStruct((cap, d), BF16),
        compiler_params=_params(("arbitrary", "arbitrary")),
        name="moe_down",
    )(exp_start, exp_nblk, act, w_down, b_down.reshape(ne, 1, d))


def _moe_route(logits):
    n, ne = logits.shape
    nk = n * TOP_K
    top_val, top_idx = lax.top_k(logits, TOP_K)
    gate = jax.nn.softmax(top_val, axis=-1).reshape(nk)
    flat_e = top_idx.reshape(nk).astype(I32)
    onehot = (flat_e[:, None] == jnp.arange(ne, dtype=I32)[None, :]).astype(I32)
    csum = jnp.cumsum(onehot, axis=0)
    rank = jnp.sum(csum * onehot, axis=1) - 1
    counts = csum[-1]
    padded = (counts + MOE_ROWS - 1) // MOE_ROWS * MOE_ROWS
    pad_end = jnp.cumsum(padded)
    pos = (pad_end - padded)[flat_e] + rank
    nblk = -(-(nk + ne * (MOE_ROWS - 1)) // MOE_ROWS)
    cap = nblk * MOE_ROWS
    slot_pair = jnp.full((cap,), -1, I32).at[pos].set(jnp.arange(nk, dtype=I32), unique_indices=True)
    slot_pair = jnp.where(slot_pair >= 0, slot_pair, jnp.arange(cap, dtype=I32) % nk)
    slot_tok = slot_pair // TOP_K
    exp_start = jnp.concatenate([pad_end - padded, pad_end[-1:]]).astype(I32)
    exp_nblk = (jnp.concatenate([padded, cap - pad_end[-1:]]) // MOE_ROWS).astype(I32)
    return slot_tok, gate.reshape(n, TOP_K), pos.reshape(n, TOP_K), exp_start, exp_nblk


def _ln2_body(h_ref, y_ref, gate_ref, g_ref, b_ref, o_ref):
    gate = gate_ref[...]
    ffn = y_ref[0].astype(F32) * gate[:, 0:1]
    for c in range(1, y_ref.shape[0]):
        ffn = ffn + y_ref[c].astype(F32) * gate[:, c:c + 1]
    o_ref[...] = _layer_norm(DN_ALPHA * h_ref[...] + ffn, g_ref[...], b_ref[...])


def _ln2(h, y4, gate, g, b, tm):
    m, d = h.shape
    kk = y4.shape[0]
    row = pl.BlockSpec((tm, d), lambda i: (i, 0))
    vec = pl.BlockSpec((1, d), lambda i: (0, 0))
    return pl.pallas_call(
        _ln2_body,
        grid=(pl.cdiv(m, tm),),
        in_specs=[row, pl.BlockSpec((kk, tm, d), lambda i: (0, i, 0)),
                  pl.BlockSpec((tm, kk), lambda i: (i, 0)), vec, vec],
        out_specs=row,
        out_shape=jax.ShapeDtypeStruct((m, d), F32),
        compiler_params=_params(("parallel",)),
        name="ln2",
    )(h, y4, gate, g.reshape(1, d), b.reshape(1, d))


def _split_w_in(w_in):
    n_dsa = DSA_Q_RANK + DSA_KV_RANK + IDX_DIM + IDX_HEADS
    w_dsa = w_in[:, :n_dsa]
    w_hq, w_hf, w_hi, w_hg = [w_in[:, n_dsa + i * HG_WIDTH:n_dsa + (i + 1) * HG_WIDTH] for i in range(4)]
    w_qig = jnp.concatenate([w_hq, w_hi, w_hg], axis=1)
    return w_dsa.astype(BF16), w_hf.astype(BF16), w_qig.astype(BF16)


def _forward(x, meta_tokens, emb_ln_g, emb_ln_b, lb_logits, w_in, q_norm_g, kv_norm_g, w_uq, w_uk, w_uv,
             w_iq, idx_k_ln_g, idx_k_ln_b, hgrn_norm_g, w_o, ln1_g, ln1_b, w_router, b_router,
             w_gate, b_gate, w_up, b_up, w_down, b_down, ln2_g, ln2_b):
    batch, seq, d = x.shape
    n = batch * seq
    k_sel = min(IDX_TOPK_MAX, seq // 4)

    lb = jnp.cumsum(jax.nn.softmax(lb_logits.astype(F32), axis=0), axis=0)[0]
    llb = jnp.log(lb).reshape(1, HG_WIDTH)
    l1m = jnp.log1p(-lb).reshape(1, HG_WIDTH)
    w_dsa, w_hf, w_qig = _split_w_in(w_in)
    w_uqb = w_uq.astype(BF16)
    w_iqb = w_iq.astype(BF16)
    w_ukt3 = w_uk.transpose(1, 2, 0).astype(BF16)
    w_uv3 = w_uv.transpose(1, 0, 2).astype(BF16)

    def mixer_inputs(tokens, tm):
        h32, h16 = _ln_rows(tokens, emb_ln_g, emb_ln_b, 256)
        dsa_in = _matmul(h16, w_dsa, F32, 512, w_dsa.shape[1], "proj_dsa")
        hf = _matmul(h16, w_hf, F32, tm, 512, "proj_hf")
        qig = _matmul(h16, w_qig, BF16, tm, 512, "proj_qig")
        prep = _dsa_prep(dsa_in, q_norm_g, kv_norm_g, idx_k_ln_g, idx_k_ln_b, w_uqb, w_ukt3, w_iqb, 256)
        return h32, hf, qig, prep

    _, hf_m, qig_m, (_, _, ckv_m, _, _) = mixer_inputs(meta_tokens.astype(F32), N_META)
    s0t = _hgrn_state0(hf_m, qig_m[:, HG_WIDTH:2 * HG_WIDTH], llb, l1m, HG_HEADS)

    h0, hf, qig, (qa, iq, ckv, ikn, iwt) = mixer_inputs(x.reshape(n, d), 1024)
    a = _dsa_attention(qa, iq, iwt, ikn, ckv, ckv_m, w_uv3, batch, seq, k_sel, 128, 512)
    g = _hgrn(hf, qig, llb, l1m, hgrn_norm_g.reshape(1, HG_WIDTH), s0t, batch, seq, 1024)

    mix = _matmul_pair(a, g, w_o.astype(BF16), F32, 1024, 512, "proj_out")
    h1, h1b, logits = _ln1_router(h0, mix, ln1_g, ln1_b, w_router, b_router, 256)

    slot_tok, gate, pos, exp_start, exp_nblk = _moe_route(logits)
    xs = jnp.take(h1b, slot_tok, axis=0, mode="clip")
    y = _moe_experts(xs, exp_start, exp_nblk, w_gate, b_gate, w_up, b_up, w_down, b_down, 512, 2048)
    y4 = jnp.take(y, pos.T.reshape(TOP_K * n), axis=0, mode="clip").reshape(TOP_K, n, d)
    out = _ln2(h1, y4, gate, ln2_g, ln2_b, 128)
    return out.reshape(batch, seq, d)


def kernel(x, meta_tokens, emb_ln_g, emb_ln_b, lb_logits, w_in, q_norm_g, kv_norm_g, w_uq, w_uk, w_uv, w_iq, idx_k_ln_g, idx_k_ln_b, hgrn_norm_g, w_o, ln1_g, ln1_b, w_router, b_router, w_gate, b_gate, w_up, b_up, w_down, b_down, ln2_g, ln2_b):
    return _forward(x, meta_tokens, emb_ln_g, emb_ln_b, lb_logits, w_in[0], q_norm_g[0], kv_norm_g[0],
                    w_uq[0], w_uk[0], w_uv[0], w_iq[0], idx_k_ln_g[0], idx_k_ln_b[0], hgrn_norm_g[0],
                    w_o[0], ln1_g[0], ln1_b[0], w_router[0], b_router[0], w_gate[0], b_gate[0],
                    w_up[0], b_up[0], w_down[0], b_down[0], ln2_g[0], ln2_b[0])
```

```python
import functools
import math

import jax
import jax.numpy as jnp
from jax import lax
from jax.experimental import pallas as pl
from jax.experimental.pallas import tpu as pltpu

F32 = jnp.float32
BF16 = jnp.bfloat16
I32 = jnp.int32

N_META = 16
DSA_HEADS = 16
DSA_HEAD_DIM = 128
DSA_Q_RANK = 768
DSA_KV_RANK = 512
IDX_HEADS = 16
IDX_DIM = 64
IDX_TOPK_MAX = 256
HG_HEADS = 16
HG_DK = 128
HG_DV = 128
HG_WIDTH = HG_HEADS * HG_DK
HG_CHUNK = 64
HG_SUB = 8
N_EXPERTS = 32
TOP_K = 4
SWIGLU_LIMIT = 7.0
SWIGLU_ALPHA = 1.702
LN_EPS = 1e-5
RMS_EPS = 1e-6
DEPTH = 1
DN_ALPHA = (2 * DEPTH) ** 0.25

V7X_VMEM_LIMIT_BYTES = 56 * 1024 * 1024
MOE_ROWS = 256
INT_MIN = -(2 ** 31)
NEG_INF = float("-inf")
LOG2E = 1.4426950408889634


def _params(sem):
    return pltpu.CompilerParams(dimension_semantics=sem, vmem_limit_bytes=V7X_VMEM_LIMIT_BYTES)


def _nt_dot(a, b):
    return lax.dot_general(a, b, (((1,), (1,)), ((), ())), preferred_element_type=F32)


def _dot(a, b):
    return jnp.dot(a, b, preferred_element_type=F32)


def _layer_norm(x, g, b):
    mu = jnp.mean(x, axis=-1, keepdims=True)
    xc = x - mu
    var = jnp.mean(xc * xc, axis=-1, keepdims=True)
    return xc * lax.rsqrt(var + LN_EPS) * g + b


def _rms_norm(x, g):
    ms = jnp.mean(x * x, axis=-1, keepdims=True)
    return x * lax.rsqrt(ms + RMS_EPS) * g


def _sigmoid(x):
    return 0.5 * jnp.tanh(0.5 * x) + 0.5


def _ln_body(x_ref, g_ref, b_ref, o32_ref, o16_ref):
    y = _layer_norm(x_ref[...].astype(F32), g_ref[...], b_ref[...])
    o32_ref[...] = y
    o16_ref[...] = y.astype(BF16)


def _ln_rows(x, g, b, tm):
    m, d = x.shape
    tm = min(tm, m)
    row = pl.BlockSpec((tm, d), lambda i: (i, 0))
    vec = pl.BlockSpec((1, d), lambda i: (0, 0))
    return pl.pallas_call(
        _ln_body,
        grid=(pl.cdiv(m, tm),),
        in_specs=[row, vec, vec],
        out_specs=[row, row],
        out_shape=[jax.ShapeDtypeStruct((m, d), F32), jax.ShapeDtypeStruct((m, d), BF16)],
        compiler_params=_params(("parallel",)),
        name="embed_ln",
    )(x, g.reshape(1, d), b.reshape(1, d))


def _mm_body(a_ref, b_ref, o_ref):
    o_ref[...] = _dot(a_ref[...], b_ref[...]).astype(o_ref.dtype)


def _matmul(a, b, out_dtype, tm, tn, name):
    m, k = a.shape
    n = b.shape[1]
    tm = min(tm, m)
    tn = min(tn, n)
    return pl.pallas_call(
        _mm_body,
        grid=(pl.cdiv(m, tm), pl.cdiv(n, tn)),
        in_specs=[pl.BlockSpec((tm, k), lambda i, j: (i, 0)),
                  pl.BlockSpec((k, tn), lambda i, j: (0, j))],
        out_specs=pl.BlockSpec((tm, tn), lambda i, j: (i, j)),
        out_shape=jax.ShapeDtypeStruct((m, n), out_dtype),
        compiler_params=_params(("parallel", "arbitrary")),
        name=name,
    )(a, b)


def _mm_pair_body(a1_ref, a2_ref, b1_ref, b2_ref, o_ref):
    o_ref[...] = (_dot(a1_ref[...], b1_ref[...]) + _dot(a2_ref[...], b2_ref[...])).astype(o_ref.dtype)


def _matmul_pair(a1, a2, b, out_dtype, tm, tn, name):
    m, k = a1.shape
    n = b.shape[1]
    assert a2.shape == (m, k) and b.shape[0] == 2 * k
    return pl.pallas_call(
        _mm_pair_body,
        grid=(pl.cdiv(m, tm), pl.cdiv(n, tn)),
        in_specs=[pl.BlockSpec((tm, k), lambda i, j: (i, 0)),
                  pl.BlockSpec((tm, k), lambda i, j: (i, 0)),
                  pl.BlockSpec((k, tn), lambda i, j: (0, j)),
                  pl.BlockSpec((k, tn), lambda i, j: (1, j))],
        out_specs=pl.BlockSpec((tm, tn), lambda i, j: (i, j)),
        out_shape=jax.ShapeDtypeStruct((m, n), out_dtype),
        compiler_params=_params(("parallel", "arbitrary")),
        name=name,
    )(a1, a2, b, b)


def _dsa_prep_body(x_ref, qg_ref, kvg_ref, ikg_ref, ikb_ref, wuq_ref, wukt_ref, wiq_ref,
                   qa_ref, iq_ref, ckv_ref, ikn_ref, iwt_ref, *, iw_scale, qk_scale):
    rq = qg_ref.shape[1]
    rkv = kvg_ref.shape[1]
    di = ikg_ref.shape[1]
    nh, dh, _ = wukt_ref.shape
    x = x_ref[...]
    cq = _rms_norm(x[:, :rq], qg_ref[...]).astype(BF16)
    ckv_ref[...] = _rms_norm(x[:, rq:rq + rkv], kvg_ref[...]).astype(BF16)
    ik = x[:, rq + rkv:rq + rkv + di]
    ikn_ref[...] = _layer_norm(ik, ikg_ref[...], ikb_ref[...]).astype(BF16)
    iwt_ref[...] = x[:, rq + rkv + di:] * iw_scale
    q = _dot(cq, wuq_ref[...]).astype(BF16)
    iq = _dot(cq, wiq_ref[...]).astype(BF16)
    for h in range(nh):
        qa_ref[h] = (_dot(q[:, h * dh:(h + 1) * dh], wukt_ref[h]) * qk_scale).astype(BF16)
        iq_ref[h] = iq[:, h * di:(h + 1) * di]


def _dsa_prep(x, q_g, kv_g, ik_g, ik_b, w_uq, w_ukt3, w_iq, tm):
    m, w = x.shape
    nh, dh, rkv = w_ukt3.shape
    rq = w_uq.shape[0]
    di = w_iq.shape[1] // nh
    nih = w - rq - rkv - di
    assert nih == nh and w_uq.shape[1] == nh * dh
    tm = min(tm, m)
    vec = lambda n: pl.BlockSpec((1, n), lambda i: (0, 0))
    full = lambda a: pl.BlockSpec(a.shape, lambda i: (0,) * a.ndim)
    return pl.pallas_call(
        functools.partial(_dsa_prep_body, iw_scale=nih ** -0.5 * di ** -0.5, qk_scale=dh ** -0.5 * LOG2E),
        grid=(pl.cdiv(m, tm),),
        in_specs=[pl.BlockSpec((tm, w), lambda i: (i, 0)),
                  vec(rq), vec(rkv), vec(di), vec(di), full(w_uq), full(w_ukt3), full(w_iq)],
        out_specs=[pl.BlockSpec((nh, tm, rkv), lambda i: (0, i, 0)),
                   pl.BlockSpec((nh, tm, di), lambda i: (0, i, 0)),
                   pl.BlockSpec((tm, rkv), lambda i: (i, 0)),
                   pl.BlockSpec((tm, di), lambda i: (i, 0)),
                   pl.BlockSpec((tm, nih), lambda i: (i, 0))],
        out_shape=[jax.ShapeDtypeStruct((nh, m, rkv), BF16),
                   jax.ShapeDtypeStruct((nh, m, di), BF16),
                   jax.ShapeDtypeStruct((m, rkv), BF16),
                   jax.ShapeDtypeStruct((m, di), BF16),
                   jax.ShapeDtypeStruct((m, nih), F32)],
        compiler_params=_params(("parallel",)),
        name="dsa_prep",
    )(x, q_g.reshape(1, rq), kv_g.reshape(1, rkv), ik_g.reshape(1, di), ik_b.reshape(1, di),
      w_uq, w_ukt3, w_iq)


def _dsa_attn_body(qa_ref, iq_ref, iwt_ref, ik_ref, ckv_ref, ckvm_ref, wuv_ref, o_ref,
                   key_sc, m_sc, l_sc, al_sc, acc_sc, p_sc, sa_sc, sb_sc, *, k_sel, n_meta, slopes):
    nh, tq, rkv = qa_ref.shape
    di = iq_ref.shape[2]
    nkb_all, _, kblk = key_sc.shape
    mpad = ckvm_ref.shape[0]
    dh = wuv_ref.shape[2]
    rows = nh * tq
    t0 = pl.program_id(1) * tq
    t_col = t0 + lax.broadcasted_iota(I32, (tq, 1), 0)
    hs = [slice(h * tq, (h + 1) * tq) for h in range(nh)]
    nkb = (t0 + tq - 1) // kblk + 1
    c_slope = [s * LOG2E for s in slopes]
    heads_per_group = 4
    slab = tq

    iq2 = iq_ref[...].reshape(rows, di)
    iw = iwt_ref[...]
    for cb in range(nkb_all):
        @pl.when(cb < nkb)
        def _():
            sc = _nt_dot(iq2, ik_ref[cb * kblk:(cb + 1) * kblk, :])
            tot = jnp.zeros((tq, kblk), F32)
            for h in range(nh):
                tot = tot + iw[:, h:h + 1] * jnp.maximum(sc[hs[h]], 0.0)
            bits = lax.bitcast_convert_type(tot, I32)
            key = jnp.where(bits < 0, bits ^ jnp.int32(0x7FFFFFFF), bits)
            col = cb * kblk + lax.broadcasted_iota(I32, (1, kblk), 1)
            key_sc[cb] = jnp.where(col <= t_col, key, jnp.int32(INT_MIN))

    lanes = 128
    half = tq // 2
    halves = (slice(0, half), slice(half, tq))

    def fold(combine, elem, init, nblk, rs):
        a = jnp.full((half, lanes), init)
        for cb in range(nblk):
            e = elem(key_sc[cb, rs, :])
            for c in range(kblk // lanes):
                a = combine(a, e[:, c * lanes:(c + 1) * lanes])
        return a

    def threshold(nblk):
        def count_ge(cand, rs):
            part = fold(jnp.add, lambda kc: (kc >= cand).astype(F32), jnp.float32(0.0), nblk, rs)
            return jnp.sum(part, axis=1, keepdims=True)

        def probe(st, mid, rs):
            lo, hi, open_f = st
            is_open = open_f > 0.0
            cnt = count_ge(mid, rs)
            ge = cnt >= float(k_sel)
            lo = jnp.where(is_open & ge, mid, lo)
            hi = jnp.where(is_open & ge & (cnt == float(k_sel)), mid, jnp.where(is_open & ~ge, mid - 1, hi))
            return lo, hi, (is_open & (lo < hi)).astype(F32)

        def midpoint(st):
            return (st[0] | st[1]) - lax.shift_right_arithmetic(st[0] ^ st[1], 1)

        states = []
        for rs in halves:
            kmax = jnp.max(fold(jnp.maximum, lambda kc: kc, jnp.int32(INT_MIN), nblk, rs), axis=1, keepdims=True)
            lo0 = jnp.full((half, 1), INT_MIN + 1, I32)
            open0 = (t_col[rs] + 1 > k_sel) & (kmax > lo0)
            first = jnp.where(kmax > INT_MIN + 1 + (1 << 24), kmax - (1 << 24), lo0 + 1)
            states.append(probe((lo0, kmax, open0.astype(F32)), first, rs))

        def n_open(sts):
            return jnp.sum(sts[0][2]) + jnp.sum(sts[1][2])

        def trip(carry):
            it, sts, _ = carry
            sts = list(sts)
            for _ in range(4):
                for i, rs in enumerate(halves):
                    sts[i] = probe(sts[i], midpoint(sts[i]), rs)
            return it + 1, tuple(sts), n_open(sts)

        out = lax.while_loop(lambda c: (c[0] < 12) & (c[2] > 0.0), trip,
                             (jnp.int32(0), tuple(states), n_open(states)))
        return jnp.concatenate([out[1][0][0], out[1][1][0]], axis=0)

    thr = lax.switch(nkb - 1, [functools.partial(threshold, c) for c in range(1, nkb_all + 1)])

    qa2 = qa_ref[...].reshape(rows, rkv)
    ckvm = ckvm_ref[...]
    sm = _nt_dot(qa2, ckvm)
    jm = lax.broadcasted_iota(I32, (1, mpad), 1)
    pos_m = (jm - n_meta - t0).astype(F32)
    for h in range(nh):
        s_h = jnp.where(jm < n_meta, sm[hs[h]] + c_slope[h] * pos_m, NEG_INF)
        m_h = jnp.max(s_h, axis=1, keepdims=True)
        p = jnp.exp2(s_h - m_h)
        m_sc[hs[h]] = m_h
        l_sc[hs[h]] = jnp.sum(p, axis=1, keepdims=True)
        p_sc[hs[h], :mpad] = p.astype(BF16)
    acc_sc[...] = _dot(p_sc[:, :mpad], ckvm)

    def kv_block(kb):
        return ckv_ref[pl.ds(pl.multiple_of(kb * kblk, kblk), kblk), :]

    def qk_into(dst, kb):
        dst[...] = _nt_dot(qa2, kv_block(jnp.minimum(kb, nkb - 1)))

    def softmax_pv(src, kb):
        pos = (kb * kblk - t0 + lax.broadcasted_iota(I32, (1, kblk), 1)).astype(F32)
        neg = jnp.where(key_sc[kb] >= thr, 0.0, NEG_INF)
        kv = kv_block(kb)
        m_all = m_sc[...]
        l_all = l_sc[...]
        for g in range(0, nh, heads_per_group):
            for h in range(g, g + heads_per_group):
                bias = neg + c_slope[h] * pos
                for r in range(0, tq, slab):
                    rs = slice(h * tq + r, h * tq + r + slab)
                    s_r = src[rs, :] + bias[r:r + slab]
                    m_old = m_all[rs]
                    m_new = jnp.maximum(m_old, jnp.max(s_r, axis=1, keepdims=True))
                    alpha = jnp.exp2(m_old - m_new)
                    p = jnp.exp2(s_r - m_new)
                    l_sc[rs] = alpha * l_all[rs] + jnp.sum(p, axis=1, keepdims=True)
                    m_sc[rs] = m_new
                    al_sc[rs] = alpha
                    p_sc[rs] = p.astype(BF16)
            gs = slice(g * tq, (g + heads_per_group) * tq)
            acc_sc[gs] = acc_sc[gs] * al_sc[gs] + _dot(p_sc[gs], kv)

    qk_into(sa_sc, 0)

    def two_blocks(i, carry):
        kb = 2 * i
        qk_into(sb_sc, kb + 1)
        softmax_pv(sa_sc, kb)

        @pl.when(kb + 1 < nkb)
        def _():
            qk_into(sa_sc, kb + 2)
            softmax_pv(sb_sc, kb + 1)
        return carry

    lax.fori_loop(0, (nkb + 1) // 2, two_blocks, 0)

    inv_l = 1.0 / l_sc[...]
    for h in range(nh):
        out_h = _dot(acc_sc[hs[h]].astype(BF16), wuv_ref[h]) * inv_l[hs[h]]
        o_ref[:, h * dh:(h + 1) * dh] = out_h.astype(o_ref.dtype)


def _dsa_attention(qa, iq, iwt, ikn, ckv, ckv_m, w_uv3, batch, seq, k_sel, tq, kblk):
    nh, m, rkv = qa.shape
    di = iq.shape[2]
    dh = w_uv3.shape[2]
    n_meta = ckv_m.shape[0]
    mpad = 128
    assert n_meta <= mpad <= kblk and seq % kblk == 0 and seq % tq == 0 and m == batch * seq
    ckv_mp = jnp.zeros((mpad, rkv), BF16).at[:n_meta].set(ckv_m)
    nq = seq // tq
    slopes = tuple(2.0 ** (-8.0 * (h + 1) / nh) for h in range(nh))
    body = functools.partial(_dsa_attn_body, k_sel=k_sel, n_meta=n_meta, slopes=slopes)
    return pl.pallas_call(
        body,
        grid=(batch, nq),
        in_specs=[pl.BlockSpec((nh, tq, rkv), lambda b, q: (0, b * nq + q, 0)),
                  pl.BlockSpec((nh, tq, di), lambda b, q: (0, b * nq + q, 0)),
                  pl.BlockSpec((tq, nh), lambda b, q: (b * nq + q, 0)),
                  pl.BlockSpec((seq, di), lambda b, q: (b, 0)),
                  pl.BlockSpec((seq, rkv), lambda b, q: (b, 0)),
                  pl.BlockSpec((mpad, rkv), lambda b, q: (0, 0)),
                  pl.BlockSpec((nh, rkv, dh), lambda b, q: (0, 0, 0))],
        out_specs=pl.BlockSpec((tq, nh * dh), lambda b, q: (b * nq + q, 0)),
        out_shape=jax.ShapeDtypeStruct((m, nh * dh), BF16),
        scratch_shapes=[pltpu.VMEM((seq // kblk, tq, kblk), I32),
                        pltpu.VMEM((nh * tq, 1), F32),
                        pltpu.VMEM((nh * tq, 1), F32),
                        pltpu.VMEM((nh * tq, 1), F32),
                        pltpu.VMEM((nh * tq, rkv), F32),
                        pltpu.VMEM((nh * tq, kblk), BF16),
                        pltpu.VMEM((nh * tq, kblk), F32),
                        pltpu.VMEM((nh * tq, kblk), F32)],
        compiler_params=_params(("parallel", "arbitrary")),
        name="dsa_attention",
    )(qa, iq, iwt, ikn, ckv, ckv_mp, w_uv3)


def _hgrn_gates(hf, llb, l1m):
    softplus = jnp.log(1.0 + jnp.exp(-jnp.abs(hf)))
    c = l1m + (jnp.minimum(hf, 0.0) - softplus)
    mx = jnp.maximum(llb, c)
    logf = mx + jnp.log(1.0 + jnp.exp(-jnp.abs(llb - c)))
    k = jnp.exp(l1m + (jnp.minimum(-hf, 0.0) - softplus))
    return logf, k


def _lower_tri(n):
    ri = lax.broadcasted_iota(I32, (n, n), 0)
    ci = lax.broadcasted_iota(I32, (n, n), 1)
    return (ci <= ri).astype(BF16)


def _cumsum_rows(x, tri):
    d = x.shape[1]
    t1 = x.astype(BF16)
    r1 = x - t1.astype(F32)
    t2 = r1.astype(BF16)
    t3 = (r1 - t2.astype(F32)).astype(BF16)
    y = _dot(tri, jnp.concatenate([t1, t2, t3], axis=1))
    return y[:, :d] + y[:, d:2 * d] + y[:, 2 * d:]


def _hgrn_state0_body(hf_ref, hi_ref, llb_ref, l1m_ref, s_ref):
    logf, k = _hgrn_gates(hf_ref[...], llb_ref[...], l1m_ref[...])
    n = logf.shape[0]
    b = _cumsum_rows(logf, _lower_tri(n))
    kd = k * jnp.exp(b[n - 1:n, :] - b)
    s_ref[0] = _dot(hi_ref[...].astype(F32).T.astype(BF16), kd.astype(BF16))


def _hgrn_state0(hf_m, hi_m, llb, l1m, nh):
    n, w = hf_m.shape
    d = w // nh
    col = lambda h: (0, h)
    return pl.pallas_call(
        _hgrn_state0_body,
        grid=(nh,),
        in_specs=[pl.BlockSpec((n, d), col), pl.BlockSpec((n, d), col),
                  pl.BlockSpec((1, d), col), pl.BlockSpec((1, d), col)],
        out_specs=pl.BlockSpec((1, d, d), lambda h: (h, 0, 0)),
        out_shape=jax.ShapeDtypeStruct((nh, d, d), F32),
        compiler_params=_params(("parallel",)),
        name="hgrn_state0",
    )(hf_m, hi_m, llb, l1m)


def _hgrn_body(hf_ref, hq_ref, hi_ref, hg_ref, llb_ref, l1m_ref, ng_ref, s0_ref, o_ref,
               st_sc, oc_sc, *, chunk, sub):
    r, d = hf_ref.shape

    @pl.when(pl.program_id(2) == 0)
    def _():
        st_sc[...] = s0_ref[0]

    logf, k = _hgrn_gates(hf_ref[...], llb_ref[...], l1m_ref[...])
    hq = hq_ref[...].astype(F32)
    q = hq * _sigmoid(hq)
    v = hi_ref[...].astype(F32)
    tri = _lower_tri(chunk)
    nchunk = r // chunk
    b = jnp.concatenate([_cumsum_rows(logf[c * chunk:(c + 1) * chunk], tri) for c in range(nchunk)],
                        axis=0)

    ones = jnp.ones((d, d), BF16)
    q3, k3, b3, v3 = [a.reshape(r // sub, sub, d) for a in (q, k, b, v)]
    row_in_sub = lax.broadcasted_iota(I32, (1, sub, 1), 1)
    o_diag = _dot((q * k).astype(BF16), ones) * v
    for off in range(1, sub):
        kr, br, vr = [pltpu.roll(a, off, 1) for a in (k3, b3, v3)]
        decay = jnp.exp(jnp.where(row_in_sub >= off, b3 - br, NEG_INF))
        prod = (q3 * kr * decay).reshape(r, d).astype(BF16)
        o_diag = o_diag + _dot(prod, ones) * vr.reshape(r, d)

    rowi = lax.broadcasted_iota(I32, (chunk, 1), 0)
    ri = lax.broadcasted_iota(I32, (chunk, chunk), 0)
    ci = lax.broadcasted_iota(I32, (chunk, chunk), 1)
    halves = []
    m = chunk // 2
    while m >= sub:
        sh = int(math.log2(2 * m))
        same_group = lax.shift_right_logical(ri, sh) == lax.shift_right_logical(ci, sh)
        halves.append((m, (rowi & (2 * m - 1)) >= m, same_group))
        m //= 2

    vb = v.astype(BF16)
    for c in range(nchunk):
        cs = slice(c * chunk, (c + 1) * chunk)
        qc, kc, bc, vc = q[cs], k[cs], b[cs], vb[cs]
        st = st_sc[...]
        blast = bc[chunk - 1:chunk, :]
        attn = jnp.zeros((chunk, chunk), F32)
        for m, upper, same_group in halves:
            anc = jnp.concatenate([jnp.broadcast_to(bc[g0 + m - 1:g0 + m, :], (2 * m, d))
                                   for g0 in range(0, chunk, 2 * m)], axis=0)
            q_m = (qc * jnp.exp(jnp.where(upper, bc - anc, NEG_INF))).astype(BF16)
            k_m = (kc * jnp.exp(jnp.where(upper, NEG_INF, anc - bc))).astype(BF16)
            attn = attn + jnp.where(same_group, _nt_dot(q_m, k_m), 0.0)
        oc_sc[cs, :] = (_nt_dot((qc * jnp.exp(bc)).astype(BF16), st.astype(BF16))
                        + _dot(attn.astype(BF16), vc))
        kd = (kc * jnp.exp(blast - bc)).astype(BF16)
        st_sc[...] = jnp.exp(blast) * st + _dot(vc.astype(F32).T.astype(BF16), kd)

    o = oc_sc[...] + o_diag
    o = _rms_norm(o, ng_ref[...])
    hg = hg_ref[...].astype(F32)
    o_ref[...] = (o * (hg * _sigmoid(hg))).astype(o_ref.dtype)


def _hgrn(hf, qig, llb, l1m, norm_g, s0t, batch, seq, rows):
    m, w = hf.shape
    nh = s0t.shape[0]
    d = w // nh
    rows = min(rows, seq)
    nblk = seq // rows
    assert seq % rows == 0 and rows % HG_CHUNK == 0 and m == batch * seq
    body = functools.partial(_hgrn_body, chunk=HG_CHUNK, sub=HG_SUB)
    blk = lambda g: pl.BlockSpec((rows, d), lambda b, h, c, g=g: (b * nblk + c, g * nh + h))
    vec = pl.BlockSpec((1, d), lambda b, h, c: (0, h))
    return pl.pallas_call(
        body,
        grid=(batch, nh, nblk),
        in_specs=[blk(0), blk(0), blk(1), blk(2), vec, vec, vec,
                  pl.BlockSpec((1, d, d), lambda b, h, c: (h, 0, 0))],
        out_specs=blk(0),
        out_shape=jax.ShapeDtypeStruct((m, w), BF16),
        scratch_shapes=[pltpu.VMEM((d, d), F32), pltpu.VMEM((rows, d), F32)],
        compiler_params=_params(("parallel", "parallel", "arbitrary")),
        name="hgrn",
    )(hf, qig, qig, qig, llb, l1m, norm_g, s0t)


def _ln1_body(h_ref, mix_ref, g_ref, b_ref, wr_ref, br_ref, o32_ref, o16_ref, lg_ref):
    y = _layer_norm(DN_ALPHA * h_ref[...] + mix_ref[...], g_ref[...], b_ref[...])
    o32_ref[...] = y
    o16_ref[...] = y.astype(BF16)
    lg_ref[...] = jnp.dot(y, wr_ref[...], precision=lax.Precision.HIGHEST,
                          preferred_element_type=F32) + br_ref[...]


def _ln1_router(h, mix, g, b, w_router, b_router, tm):
    m, d = h.shape
    ne = w_router.shape[1]
    row = pl.BlockSpec((tm, d), lambda i: (i, 0))
    vec = pl.BlockSpec((1, d), lambda i: (0, 0))
    return pl.pallas_call(
        _ln1_body,
        grid=(pl.cdiv(m, tm),),
        in_specs=[row, row, vec, vec,
                  pl.BlockSpec((d, ne), lambda i: (0, 0)), pl.BlockSpec((1, ne), lambda i: (0, 0))],
        out_specs=[row, row, pl.BlockSpec((tm, ne), lambda i: (i, 0))],
        out_shape=[jax.ShapeDtypeStruct((m, d), F32), jax.ShapeDtypeStruct((m, d), BF16),
                   jax.ShapeDtypeStruct((m, ne), F32)],
        compiler_params=_params(("parallel",)),
        name="ln1_router",
    )(h, mix, g.reshape(1, d), b.reshape(1, d), w_router, b_router.reshape(1, ne))


def _stream_expert_rows(start_ref, nblk_ref, x_hbm, o_hbm, x_buf, o_buf, x_sem, o_sem, setup, compute):
    e = pl.program_id(0)
    j = pl.program_id(1)
    rows, tn = o_buf.shape[1], o_buf.shape[2]
    base = start_ref[e]
    nb = nblk_ref[e]

    def x_copy(i, slot):
        r0 = pl.multiple_of(base + i * rows, rows)
        return pltpu.make_async_copy(x_hbm.at[pl.ds(r0, rows)], x_buf.at[slot], x_sem.at[slot])

    def o_copy(i, slot):
        r0 = pl.multiple_of(base + i * rows, rows)
        c0 = pl.multiple_of(j * tn, tn)
        return pltpu.make_async_copy(o_buf.at[slot], o_hbm.at[pl.ds(r0, rows), pl.ds(c0, tn)], o_sem.at[slot])

    @pl.when(nb > 0)
    def _():
        x_copy(0, 0).start(priority=1)

    setup()

    def row_block(i, carry):
        slot = i & 1
        x_copy(i, slot).wait()

        @pl.when(i + 1 < nb)
        def _():
            x_copy(i + 1, 1 - slot).start(priority=1)

        @pl.when(i >= 2)
        def _():
            o_copy(i - 2, slot).wait()

        o_buf[slot] = compute(x_buf[slot]).astype(o_buf.dtype)
        o_copy(i, slot).start(priority=1)
        return carry

    lax.fori_loop(0, nb, row_block, 0)

    @pl.when(nb >= 2)
    def _():
        o_copy(nb - 2, nb & 1).wait()

    @pl.when(nb >= 1)
    def _():
        o_copy(nb - 1, (nb - 1) & 1).wait()

    n_exp = pl.num_programs(0)

    @pl.when(e == n_exp - 1)
    def _():
        o_buf[0] = jnp.zeros(o_buf.shape[1:], o_buf.dtype)

        def zero_block(t, carry):
            r0 = pl.multiple_of(start_ref[n_exp] + t * rows, rows)
            c0 = pl.multiple_of(j * tn, tn)
            cp = pltpu.make_async_copy(o_buf.at[0], o_hbm.at[pl.ds(r0, rows), pl.ds(c0, tn)], o_sem.at[0])
            cp.start()
            cp.wait()
            return carry

        lax.fori_loop(0, nblk_ref[n_exp], zero_block, 0)


def _moe_up_body(start_ref, nblk_ref, x_hbm, wg_ref, bg_ref, wu_ref, bu_ref, o_hbm,
                 wg_sc, wu_sc, x_buf, o_buf, x_sem, o_sem):
    def cast_weights():
        wg_sc[...] = wg_ref[0].astype(BF16)
        wu_sc[...] = wu_ref[0].astype(BF16)

    def swiglu(x):
        a = jnp.minimum(_dot(x, wg_sc[...]) + bg_ref[0], SWIGLU_LIMIT)
        u = jnp.clip(_dot(x, wu_sc[...]) + bu_ref[0], -SWIGLU_LIMIT, SWIGLU_LIMIT)
        return (u + 1.0) * a * _sigmoid(SWIGLU_ALPHA * a)

    _stream_expert_rows(start_ref, nblk_ref, x_hbm, o_hbm, x_buf, o_buf, x_sem, o_sem, cast_weights, swiglu)


def _moe_down_body(start_ref, nblk_ref, a_hbm, wd_ref, bd_ref, o_hbm, wd_sc, a_buf, o_buf, a_sem, o_sem):
    def cast_weights():
        wd_sc[...] = wd_ref[0].astype(BF16)

    def down(a):
        return _dot(a, wd_sc[...]) + bd_ref[0]

    _stream_expert_rows(start_ref, nblk_ref, a_hbm, o_hbm, a_buf, o_buf, a_sem, o_sem, cast_weights, down)


def _moe_experts(xs, exp_start, exp_nblk, w_gate, b_gate, w_up, b_up, w_down, b_down, tn_up, tn_down):
    cap, d = xs.shape
    ne, _, f = w_gate.shape
    tn_up = min(tn_up, f)
    tn_down = min(tn_down, d)
    any_space = pl.BlockSpec(memory_space=pl.ANY)
    stream_scratch = lambda k, tn: [pltpu.VMEM((2, MOE_ROWS, k), BF16), pltpu.VMEM((2, MOE_ROWS, tn), BF16),
                                    pltpu.SemaphoreType.DMA((2,)), pltpu.SemaphoreType.DMA((2,))]
    w_tile = lambda k, tn: pl.BlockSpec((1, k, tn), lambda e, j, st, nb: (e, 0, j))
    act = pl.pallas_call(
        _moe_up_body,
        grid_spec=pltpu.PrefetchScalarGridSpec(
            num_scalar_prefetch=2,
            grid=(ne, f // tn_up),
            in_specs=[any_space, w_tile(d, tn_up), w_tile(1, tn_up), w_tile(d, tn_up), w_tile(1, tn_up)],
            out_specs=any_space,
            scratch_shapes=[pltpu.VMEM((d, tn_up), BF16), pltpu.VMEM((d, tn_up), BF16)]
            + stream_scratch(d, tn_up)),
        out_shape=jax.ShapeDtypeStruct((cap, f), BF16),
        compiler_params=_params(("arbitrary", "arbitrary")),
        name="moe_up",
    )(exp_start, exp_nblk, xs, w_gate, b_gate.reshape(ne, 1, f), w_up, b_up.reshape(ne, 1, f))
    return pl.pallas_call(
        _moe_down_body,
        grid_spec=pltpu.PrefetchScalarGridSpec(
            num_scalar_prefetch=2,
            grid=(ne, d // tn_down),
            in_specs=[any_space, w_tile(f, tn_down), w_tile(1, tn_down)],
            out_specs=any_space,
            scratch_shapes=[pltpu.VMEM((f, tn_down), BF16)] + stream_scratch(f, tn_down)),
        out_shape=jax.ShapeDtypeStruct((cap, d), BF16),
        compiler_params=_params(("arbitrary", "arbitrary")),
        name="moe_down",
    )(exp_start, exp_nblk, act, w_down, b_down.reshape(ne, 1, d))


def _moe_route(logits):
    n, ne = logits.shape
    nk = n * TOP_K
    top_val, top_idx = lax.top_k(logits, TOP_K)
    gate = jax.nn.softmax(top_val, axis=-1).reshape(nk)
    flat_e = top_idx.reshape(nk).astype(I32)
    onehot = (flat_e[:, None] == jnp.arange(ne, dtype=I32)[None, :]).astype(I32)
    csum = jnp.cumsum(onehot, axis=0)
    rank = jnp.sum(csum * onehot, axis=1) - 1
    counts = csum[-1]
    padded = (counts + MOE_ROWS - 1) // MOE_ROWS * MOE_ROWS
    pad_end = jnp.cumsum(padded)
    pos = (pad_end - padded)[flat_e] + rank
    nblk = -(-(nk + ne * (MOE_ROWS - 1)) // MOE_ROWS)
    cap = nblk * MOE_ROWS
    slot_pair = jnp.full((cap,), -1, I32).at[pos].set(jnp.arange(nk, dtype=I32), unique_indices=True)
    slot_pair = jnp.where(slot_pair >= 0, slot_pair, jnp.arange(cap, dtype=I32) % nk)
    slot_tok = slot_pair // TOP_K
    exp_start = jnp.concatenate([pad_end - padded, pad_end[-1:]]).astype(I32)
    exp_nblk = (jnp.concatenate([padded, cap - pad_end[-1:]]) // MOE_ROWS).astype(I32)
    return slot_tok, gate.reshape(n, TOP_K), pos.reshape(n, TOP_K), exp_start, exp_nblk


def _ln2_body(h_ref, y_ref, gate_ref, g_ref, b_ref, o_ref):
    gate = gate_ref[...]
    ffn = y_ref[0].astype(F32) * gate[:, 0:1]
    for c in range(1, y_ref.shape[0]):
        ffn = ffn + y_ref[c].astype(F32) * gate[:, c:c + 1]
    o_ref[...] = _layer_norm(DN_ALPHA * h_ref[...] + ffn, g_ref[...], b_ref[...])


def _ln2(h, y4, gate, g, b, tm):
    m, d = h.shape
    kk = y4.shape[0]
    row = pl.BlockSpec((tm, d), lambda i: (i, 0))
    vec = pl.BlockSpec((1, d), lambda i: (0, 0))
    return pl.pallas_call(
        _ln2_body,
        grid=(pl.cdiv(m, tm),),
        in_specs=[row, pl.BlockSpec((kk, tm, d), lambda i: (0, i, 0)),
                  pl.BlockSpec((tm, kk), lambda i: (i, 0)), vec, vec],
        out_specs=row,
        out_shape=jax.ShapeDtypeStruct((m, d), F32),
        compiler_params=_params(("parallel",)),
        name="ln2",
    )(h, y4, gate, g.reshape(1, d), b.reshape(1, d))


def _split_w_in(w_in):
    n_dsa = DSA_Q_RANK + DSA_KV_RANK + IDX_DIM + IDX_HEADS
    w_dsa = w_in[:, :n_dsa]
    w_hq, w_hf, w_hi, w_hg = [w_in[:, n_dsa + i * HG_WIDTH:n_dsa + (i + 1) * HG_WIDTH] for i in range(4)]
    w_qig = jnp.concatenate([w_hq, w_hi, w_hg], axis=1)
    return w_dsa.astype(BF16), w_hf.astype(BF16), w_qig.astype(BF16)


def _forward(x, meta_tokens, emb_ln_g, emb_ln_b, lb_logits, w_in, q_norm_g, kv_norm_g, w_uq, w_uk, w_uv,
             w_iq, idx_k_ln_g, idx_k_ln_b, hgrn_norm_g, w_o, ln1_g, ln1_b, w_router, b_router,
             w_gate, b_gate, w_up, b_up, w_down, b_down, ln2_g, ln2_b):
    batch, seq, d = x.shape
    n = batch * seq
    k_sel = min(IDX_TOPK_MAX, seq // 4)

    lb = jnp.cumsum(jax.nn.softmax(lb_logits.astype(F32), axis=0), axis=0)[0]
    llb = jnp.log(lb).reshape(1, HG_WIDTH)
    l1m = jnp.log1p(-lb).reshape(1, HG_WIDTH)
    w_dsa, w_hf, w_qig = _split_w_in(w_in)
    w_uqb = w_uq.astype(BF16)
    w_iqb = w_iq.astype(BF16)
    w_ukt3 = w_uk.transpose(1, 2, 0).astype(BF16)
    w_uv3 = w_uv.transpose(1, 0, 2).astype(BF16)

    def mixer_inputs(tokens, tm):
        h32, h16 = _ln_rows(tokens, emb_ln_g, emb_ln_b, 256)
        dsa_in = _matmul(h16, w_dsa, F32, 512, w_dsa.shape[1], "proj_dsa")
        hf = _matmul(h16, w_hf, F32, tm, 512, "proj_hf")
        qig = _matmul(h16, w_qig, BF16, tm, 512, "proj_qig")
        prep = _dsa_prep(dsa_in, q_norm_g, kv_norm_g, idx_k_ln_g, idx_k_ln_b, w_uqb, w_ukt3, w_iqb, 256)
        return h32, hf, qig, prep

    _, hf_m, qig_m, (_, _, ckv_m, _, _) = mixer_inputs(meta_tokens.astype(F32), N_META)
    s0t = _hgrn_state0(hf_m, qig_m[:, HG_WIDTH:2 * HG_WIDTH], llb, l1m, HG_HEADS)

    h0, hf, qig, (qa, iq, ckv, ikn, iwt) = mixer_inputs(x.reshape(n, d), 1024)
    a = _dsa_attention(qa, iq, iwt, ikn, ckv, ckv_m, w_uv3, batch, seq, k_sel, 128, 512)
    g = _hgrn(hf, qig, llb, l1m, hgrn_norm_g.reshape(1, HG_WIDTH), s0t, batch, seq, 1024)

    mix = _matmul_pair(a, g, w_o.astype(BF16), F32, 1024, 512, "proj_out")
    h1, h1b, logits = _ln1_router(h0, mix, ln1_g, ln1_b, w_router, b_router, 256)

    slot_tok, gate, pos, exp_start, exp_nblk = _moe_route(logits)
    xs = jnp.take(h1b, slot_tok, axis=0, mode="clip")
    y = _moe_experts(xs, exp_start, exp_nblk, w_gate, b_gate, w_up, b_up, w_down, b_down, 512, 2048)
    y4 = jnp.take(y, pos.T.reshape(TOP_K * n), axis=0, mode="clip").reshape(TOP_K, n, d)
    out = _ln2(h1, y4, gate, ln2_g, ln2_b, 128)
    return out.reshape(batch, seq, d)


def kernel(x, meta_tokens, emb_ln_g, emb_ln_b, lb_logits, w_in, q_norm_g, kv_norm_g, w_uq, w_uk, w_uv, w_iq, idx_k_ln_g, idx_k_ln_b, hgrn_norm_g, w_o, ln1_g, ln1_b, w_router, b_router, w_gate, b_gate, w_up, b_up, w_down, b_down, ln2_g, ln2_b):
    return _forward(x, meta_tokens, emb_ln_g, emb_ln_b, lb_logits, w_in[0], q_norm_g[0], kv_norm_g[0],
                    w_uq[0], w_uk[0], w_uv[0], w_iq[0], idx_k_ln_g[0], idx_k_ln_b[0], hgrn_norm_g[0],
                    w_o[0], ln1_g[0], ln1_b[0], w_router[0], b_router[0], w_gate[0], b_gate[0],
                    w_up[0], b_up[0], w_down[0], b_down[0], ln2_g[0], ln2_b[0])
```

```python
import functools
import math

import jax
import jax.numpy as jnp
from jax import lax
from jax.experimental import pallas as pl
from jax.experimental.pallas import tpu as pltpu

F32 = jnp.float32
BF16 = jnp.bfloat16
I32 = jnp.int32

N_META = 16
DSA_HEADS = 16
DSA_HEAD_DIM = 128
DSA_Q_RANK = 768
DSA_KV_RANK = 512
IDX_HEADS = 16
IDX_DIM = 64
IDX_TOPK_MAX = 256
HG_HEADS = 16
HG_DK = 128
HG_DV = 128
HG_WIDTH = HG_HEADS * HG_DK
HG_CHUNK = 64
HG_SUB = 8
N_EXPERTS = 32
TOP_K = 4
SWIGLU_LIMIT = 7.0
SWIGLU_ALPHA = 1.702
LN_EPS = 1e-5
RMS_EPS = 1e-6
DEPTH = 1
DN_ALPHA = (2 * DEPTH) ** 0.25

V7X_VMEM_LIMIT_BYTES = 56 * 1024 * 1024
MOE_ROWS = 512
INT_MIN = -(2 ** 31)
NEG_INF = float("-inf")
LOG2E = 1.4426950408889634


def _params(sem):
    return pltpu.CompilerParams(dimension_semantics=sem, vmem_limit_bytes=V7X_VMEM_LIMIT_BYTES)


def _nt_dot(a, b):
    return lax.dot_general(a, b, (((1,), (1,)), ((), ())), preferred_element_type=F32)


def _dot(a, b):
    return jnp.dot(a, b, preferred_element_type=F32)


def _layer_norm(x, g, b):
    mu = jnp.mean(x, axis=-1, keepdims=True)
    xc = x - mu
    var = jnp.mean(xc * xc, axis=-1, keepdims=True)
    return xc * lax.rsqrt(var + LN_EPS) * g + b


def _rms_norm(x, g):
    ms = jnp.mean(x * x, axis=-1, keepdims=True)
    return x * lax.rsqrt(ms + RMS_EPS) * g


def _sigmoid(x):
    return 0.5 * jnp.tanh(0.5 * x) + 0.5


def _ln_body(x_ref, g_ref, b_ref, o32_ref, o16_ref):
    y = _layer_norm(x_ref[...].astype(F32), g_ref[...], b_ref[...])
    o32_ref[...] = y
    o16_ref[...] = y.astype(BF16)


def _ln_rows(x, g, b, tm):
    m, d = x.shape
    tm = min(tm, m)
    row = pl.BlockSpec((tm, d), lambda i: (i, 0))
    vec = pl.BlockSpec((1, d), lambda i: (0, 0))
    return pl.pallas_call(
        _ln_body,
        grid=(pl.cdiv(m, tm),),
        in_specs=[row, vec, vec],
        out_specs=[row, row],
        out_shape=[jax.ShapeDtypeStruct((m, d), F32), jax.ShapeDtypeStruct((m, d), BF16)],
        compiler_params=_params(("parallel",)),
        name="embed_ln",
    )(x, g.reshape(1, d), b.reshape(1, d))


def _mm_body(a_ref, b_ref, o_ref):
    o_ref[...] = _dot(a_ref[...], b_ref[...]).astype(o_ref.dtype)


def _matmul(a, b, out_dtype, tm, tn, name):
    m, k = a.shape
    n = b.shape[1]
    tm = min(tm, m)
    tn = min(tn, n)
    return pl.pallas_call(
        _mm_body,
        grid=(pl.cdiv(m, tm), pl.cdiv(n, tn)),
        in_specs=[pl.BlockSpec((tm, k), lambda i, j: (i, 0)),
                  pl.BlockSpec((k, tn), lambda i, j: (0, j))],
        out_specs=pl.BlockSpec((tm, tn), lambda i, j: (i, j)),
        out_shape=jax.ShapeDtypeStruct((m, n), out_dtype),
        compiler_params=_params(("parallel", "arbitrary")),
        name=name,
    )(a, b)


def _mm_pair_body(a1_ref, a2_ref, b1_ref, b2_ref, o_ref):
    o_ref[...] = (_dot(a1_ref[...], b1_ref[...]) + _dot(a2_ref[...], b2_ref[...])).astype(o_ref.dtype)


def _matmul_pair(a1, a2, b, out_dtype, tm, tn, name):
    m, k = a1.shape
    n = b.shape[1]
    assert a2.shape == (m, k) and b.shape[0] == 2 * k
    return pl.pallas_call(
        _mm_pair_body,
        grid=(pl.cdiv(m, tm), pl.cdiv(n, tn)),
        in_specs=[pl.BlockSpec((tm, k), lambda i, j: (i, 0)),
                  pl.BlockSpec((tm, k), lambda i, j: (i, 0)),
                  pl.BlockSpec((k, tn), lambda i, j: (0, j)),
                  pl.BlockSpec((k, tn), lambda i, j: (1, j))],
        out_specs=pl.BlockSpec((tm, tn), lambda i, j: (i, j)),
        out_shape=jax.ShapeDtypeStruct((m, n), out_dtype),
        compiler_params=_params(("parallel", "arbitrary")),
        name=name,
    )(a1, a2, b, b)


def _dsa_prep_body(x_ref, qg_ref, kvg_ref, ikg_ref, ikb_ref, wuq_ref, wukt_ref, wiq_ref,
                   qa_ref, iq_ref, ckv_ref, ikn_ref, iwt_ref, *, iw_scale, qk_scale):
    rq = qg_ref.shape[1]
    rkv = kvg_ref.shape[1]
    di = ikg_ref.shape[1]
    nh, dh, _ = wukt_ref.shape
    x = x_ref[...]
    cq = _rms_norm(x[:, :rq], qg_ref[...]).astype(BF16)
    ckv_ref[...] = _rms_norm(x[:, rq:rq + rkv], kvg_ref[...]).astype(BF16)
    ik = x[:, rq + rkv:rq + rkv + di]
    ikn_ref[...] = _layer_norm(ik, ikg_ref[...], ikb_ref[...]).astype(BF16)
    iwt_ref[...] = x[:, rq + rkv + di:] * iw_scale
    q = _dot(cq, wuq_ref[...]).astype(BF16)
    iq = _dot(cq, wiq_ref[...]).astype(BF16)
    for h in range(nh):
        qa_ref[h] = (_dot(q[:, h * dh:(h + 1) * dh], wukt_ref[h]) * qk_scale).astype(BF16)
        iq_ref[h] = iq[:, h * di:(h + 1) * di]


def _dsa_prep(x, q_g, kv_g, ik_g, ik_b, w_uq, w_ukt3, w_iq, tm):
    m, w = x.shape
    nh, dh, rkv = w_ukt3.shape
    rq = w_uq.shape[0]
    di = w_iq.shape[1] // nh
    nih = w - rq - rkv - di
    assert nih == nh and w_uq.shape[1] == nh * dh
    tm = min(tm, m)
    vec = lambda n: pl.BlockSpec((1, n), lambda i: (0, 0))
    full = lambda a: pl.BlockSpec(a.shape, lambda i: (0,) * a.ndim)
    return pl.pallas_call(
        functools.partial(_dsa_prep_body, iw_scale=nih ** -0.5 * di ** -0.5, qk_scale=dh ** -0.5 * LOG2E),
        grid=(pl.cdiv(m, tm),),
        in_specs=[pl.BlockSpec((tm, w), lambda i: (i, 0)),
                  vec(rq), vec(rkv), vec(di), vec(di), full(w_uq), full(w_ukt3), full(w_iq)],
        out_specs=[pl.BlockSpec((nh, tm, rkv), lambda i: (0, i, 0)),
                   pl.BlockSpec((nh, tm, di), lambda i: (0, i, 0)),
                   pl.BlockSpec((tm, rkv), lambda i: (i, 0)),
                   pl.BlockSpec((tm, di), lambda i: (i, 0)),
                   pl.BlockSpec((tm, nih), lambda i: (i, 0))],
        out_shape=[jax.ShapeDtypeStruct((nh, m, rkv), BF16),
                   jax.ShapeDtypeStruct((nh, m, di), BF16),
                   jax.ShapeDtypeStruct((m, rkv), BF16),
                   jax.ShapeDtypeStruct((m, di), BF16),
                   jax.ShapeDtypeStruct((m, nih), F32)],
        compiler_params=_params(("parallel",)),
        name="dsa_prep",
    )(x, q_g.reshape(1, rq), kv_g.reshape(1, rkv), ik_g.reshape(1, di), ik_b.reshape(1, di),
      w_uq, w_ukt3, w_iq)


def _dsa_attn_body(qa_ref, iq_ref, iwt_ref, ik_ref, ckv_ref, ckvm_ref, wuv_ref, o_ref,
                   key_sc, m_sc, l_sc, al_sc, acc_sc, p_sc, sa_sc, sb_sc, *, k_sel, n_meta, slopes):
    nh, tq, rkv = qa_ref.shape
    di = iq_ref.shape[2]
    nkb_all, _, kblk = key_sc.shape
    mpad = ckvm_ref.shape[0]
    dh = wuv_ref.shape[2]
    rows = nh * tq
    t0 = pl.program_id(1) * tq
    t_col = t0 + lax.broadcasted_iota(I32, (tq, 1), 0)
    hs = [slice(h * tq, (h + 1) * tq) for h in range(nh)]
    nkb = (t0 + tq - 1) // kblk + 1
    c_slope = [s * LOG2E for s in slopes]
    heads_per_group = 4
    slab = tq

    iq2 = iq_ref[...].reshape(rows, di)
    iw = iwt_ref[...]
    for cb in range(nkb_all):
        @pl.when(cb < nkb)
        def _():
            sc = _nt_dot(iq2, ik_ref[cb * kblk:(cb + 1) * kblk, :])
            tot = jnp.zeros((tq, kblk), F32)
            for h in range(nh):
                tot = tot + iw[:, h:h + 1] * jnp.maximum(sc[hs[h]], 0.0)
            bits = lax.bitcast_convert_type(tot, I32)
            key = jnp.where(bits < 0, bits ^ jnp.int32(0x7FFFFFFF), bits)
            col = cb * kblk + lax.broadcasted_iota(I32, (1, kblk), 1)
            key_sc[cb] = jnp.where(col <= t_col, key, jnp.int32(INT_MIN))

    lanes = 128
    half = tq // 2
    halves = (slice(0, half), slice(half, tq))

    def fold(combine, elem, init, nblk, rs):
        a = jnp.full((half, lanes), init)
        for cb in range(nblk):
            e = elem(key_sc[cb, rs, :])
            for c in range(kblk // lanes):
                a = combine(a, e[:, c * lanes:(c + 1) * lanes])
        return a

    def threshold(nblk):
        def count_ge(cand, rs):
            part = fold(jnp.add, lambda kc: (kc >= cand).astype(F32), jnp.float32(0.0), nblk, rs)
            return jnp.sum(part, axis=1, keepdims=True)

        def probe(st, mid, rs):
            lo, hi, open_f = st
            is_open = open_f > 0.0
            cnt = count_ge(mid, rs)
            ge = cnt >= float(k_sel)
            lo = jnp.where(is_open & ge, mid, lo)
            hi = jnp.where(is_open & ge & (cnt == float(k_sel)), mid, jnp.where(is_open & ~ge, mid - 1, hi))
            return lo, hi, (is_open & (lo < hi)).astype(F32)

        def midpoint(st):
            return (st[0] | st[1]) - lax.shift_right_arithmetic(st[0] ^ st[1], 1)

        states = []
        for rs in halves:
            kmax = jnp.max(fold(jnp.maximum, lambda kc: kc, jnp.int32(INT_MIN), nblk, rs), axis=1, keepdims=True)
            lo0 = jnp.full((half, 1), INT_MIN + 1, I32)
            open0 = (t_col[rs] + 1 > k_sel) & (kmax > lo0)
            first = jnp.where(kmax > INT_MIN + 1 + (1 << 24), kmax - (1 << 24), lo0 + 1)
            states.append(probe((lo0, kmax, open0.astype(F32)), first, rs))

        def n_open(sts):
            return jnp.sum(sts[0][2]) + jnp.sum(sts[1][2])

        def trip(carry):
            it, sts, _ = carry
            sts = list(sts)
            for _ in range(4):
                for i, rs in enumerate(halves):
                    sts[i] = probe(sts[i], midpoint(sts[i]), rs)
            return it + 1, tuple(sts), n_open(sts)

        out = lax.while_loop(lambda c: (c[0] < 12) & (c[2] > 0.0), trip,
                             (jnp.int32(0), tuple(states), n_open(states)))
        return jnp.concatenate([out[1][0][0], out[1][1][0]], axis=0)

    thr = lax.switch(nkb - 1, [functools.partial(threshold, c) for c in range(1, nkb_all + 1)])

    qa2 = qa_ref[...].reshape(rows, rkv)
    ckvm = ckvm_ref[...]
    sm = _nt_dot(qa2, ckvm)
    jm = lax.broadcasted_iota(I32, (1, mpad), 1)
    pos_m = (jm - n_meta - t0).astype(F32)
    for h in range(nh):
        s_h = jnp.where(jm < n_meta, sm[hs[h]] + c_slope[h] * pos_m, NEG_INF)
        m_h = jnp.max(s_h, axis=1, keepdims=True)
        p = jnp.exp2(s_h - m_h)
        m_sc[hs[h]] = m_h
        l_sc[hs[h]] = jnp.sum(p, axis=1, keepdims=True)
        p_sc[hs[h], :mpad] = p.astype(BF16)
    acc_sc[...] = _dot(p_sc[:, :mpad], ckvm)

    def kv_block(kb):
        return ckv_ref[pl.ds(pl.multiple_of(kb * kblk, kblk), kblk), :]

    def qk_into(dst, kb):
        dst[...] = _nt_dot(qa2, kv_block(jnp.minimum(kb, nkb - 1)))

    def softmax_pv(src, kb):
        pos = (kb * kblk - t0 + lax.broadcasted_iota(I32, (1, kblk), 1)).astype(F32)
        neg = jnp.where(key_sc[kb] >= thr, 0.0, NEG_INF)
        kv = kv_block(kb)
        m_all = m_sc[...]
        l_all = l_sc[...]
        for g in range(0, nh, heads_per_group):
            for h in range(g, g + heads_per_group):
                bias = neg + c_slope[h] * pos
                for r in range(0, tq, slab):
                    rs = slice(h * tq + r, h * tq + r + slab)
                    s_r = src[rs, :] + bias[r:r + slab]
                    m_old = m_all[rs]
                    m_new = jnp.maximum(m_old, jnp.max(s_r, axis=1, keepdims=True))
                    alpha = jnp.exp2(m_old - m_new)
                    p = jnp.exp2(s_r - m_new)
                    l_sc[rs] = alpha * l_all[rs] + jnp.sum(p, axis=1, keepdims=True)
                    m_sc[rs] = m_new
                    al_sc[rs] = alpha
                    p_sc[rs] = p.astype(BF16)
            gs = slice(g * tq, (g + heads_per_group) * tq)
            acc_sc[gs] = acc_sc[gs] * al_sc[gs] + _dot(p_sc[gs], kv)

    qk_into(sa_sc, 0)

    def two_blocks(i, carry):
        kb = 2 * i
        qk_into(sb_sc, kb + 1)
        softmax_pv(sa_sc, kb)

        @pl.when(kb + 1 < nkb)
        def _():
            qk_into(sa_sc, kb + 2)
            softmax_pv(sb_sc, kb + 1)
        return carry

    lax.fori_loop(0, (nkb + 1) // 2, two_blocks, 0)

    inv_l = 1.0 / l_sc[...]
    for h in range(nh):
        out_h = _dot(acc_sc[hs[h]].astype(BF16), wuv_ref[h]) * inv_l[hs[h]]
        o_ref[:, h * dh:(h + 1) * dh] = out_h.astype(o_ref.dtype)


def _dsa_attention(qa, iq, iwt, ikn, ckv, ckv_m, w_uv3, batch, seq, k_sel, tq, kblk):
    nh, m, rkv = qa.shape
    di = iq.shape[2]
    dh = w_uv3.shape[2]
    n_meta = ckv_m.shape[0]
    mpad = 128
    assert n_meta <= mpad <= kblk and seq % kblk == 0 and seq % tq == 0 and m == batch * seq
    ckv_mp = jnp.zeros((mpad, rkv), BF16).at[:n_meta].set(ckv_m)
    nq = seq // tq
    slopes = tuple(2.0 ** (-8.0 * (h + 1) / nh) for h in range(nh))
    body = functools.partial(_dsa_attn_body, k_sel=k_sel, n_meta=n_meta, slopes=slopes)
    return pl.pallas_call(
        body,
        grid=(batch, nq),
        in_specs=[pl.BlockSpec((nh, tq, rkv), lambda b, q: (0, b * nq + q, 0)),
                  pl.BlockSpec((nh, tq, di), lambda b, q: (0, b * nq + q, 0)),
                  pl.BlockSpec((tq, nh), lambda b, q: (b * nq + q, 0)),
                  pl.BlockSpec((seq, di), lambda b, q: (b, 0)),
                  pl.BlockSpec((seq, rkv), lambda b, q: (b, 0)),
                  pl.BlockSpec((mpad, rkv), lambda b, q: (0, 0)),
                  pl.BlockSpec((nh, rkv, dh), lambda b, q: (0, 0, 0))],
        out_specs=pl.BlockSpec((tq, nh * dh), lambda b, q: (b * nq + q, 0)),
        out_shape=jax.ShapeDtypeStruct((m, nh * dh), BF16),
        scratch_shapes=[pltpu.VMEM((seq // kblk, tq, kblk), I32),
                        pltpu.VMEM((nh * tq, 1), F32),
                        pltpu.VMEM((nh * tq, 1), F32),
                        pltpu.VMEM((nh * tq, 1), F32),
                        pltpu.VMEM((nh * tq, rkv), F32),
                        pltpu.VMEM((nh * tq, kblk), BF16),
                        pltpu.VMEM((nh * tq, kblk), F32),
                        pltpu.VMEM((nh * tq, kblk), F32)],
        compiler_params=_params(("parallel", "arbitrary")),
        name="dsa_attention",
    )(qa, iq, iwt, ikn, ckv, ckv_mp, w_uv3)


def _hgrn_gates(hf, llb, l1m):
    softplus = jnp.log(1.0 + jnp.exp(-jnp.abs(hf)))
    c = l1m + (jnp.minimum(hf, 0.0) - softplus)
    mx = jnp.maximum(llb, c)
    logf = mx + jnp.log(1.0 + jnp.exp(-jnp.abs(llb - c)))
    k = jnp.exp(l1m + (jnp.minimum(-hf, 0.0) - softplus))
    return logf, k


def _lower_tri(n):
    ri = lax.broadcasted_iota(I32, (n, n), 0)
    ci = lax.broadcasted_iota(I32, (n, n), 1)
    return (ci <= ri).astype(BF16)


def _cumsum_rows(x, tri):
    d = x.shape[1]
    t1 = x.astype(BF16)
    r1 = x - t1.astype(F32)
    t2 = r1.astype(BF16)
    t3 = (r1 - t2.astype(F32)).astype(BF16)
    y = _dot(tri, jnp.concatenate([t1, t2, t3], axis=1))
    return y[:, :d] + y[:, d:2 * d] + y[:, 2 * d:]


def _hgrn_state0_body(hf_ref, hi_ref, llb_ref, l1m_ref, s_ref):
    logf, k = _hgrn_gates(hf_ref[...], llb_ref[...], l1m_ref[...])
    n = logf.shape[0]
    b = _cumsum_rows(logf, _lower_tri(n))
    kd = k * jnp.exp(b[n - 1:n, :] - b)
    s_ref[0] = _dot(hi_ref[...].astype(F32).T.astype(BF16), kd.astype(BF16))


def _hgrn_state0(hf_m, hi_m, llb, l1m, nh):
    n, w = hf_m.shape
    d = w // nh
    col = lambda h: (0, h)
    return pl.pallas_call(
        _hgrn_state0_body,
        grid=(nh,),
        in_specs=[pl.BlockSpec((n, d), col), pl.BlockSpec((n, d), col),
                  pl.BlockSpec((1, d), col), pl.BlockSpec((1, d), col)],
        out_specs=pl.BlockSpec((1, d, d), lambda h: (h, 0, 0)),
        out_shape=jax.ShapeDtypeStruct((nh, d, d), F32),
        compiler_params=_params(("parallel",)),
        name="hgrn_state0",
    )(hf_m, hi_m, llb, l1m)


def _hgrn_body(hf_ref, hq_ref, hi_ref, hg_ref, llb_ref, l1m_ref, ng_ref, s0_ref, o_ref,
               st_sc, oc_sc, *, chunk, sub):
    r, d = hf_ref.shape

    @pl.when(pl.program_id(2) == 0)
    def _():
        st_sc[...] = s0_ref[0]

    logf, k = _hgrn_gates(hf_ref[...], llb_ref[...], l1m_ref[...])
    hq = hq_ref[...].astype(F32)
    q = hq * _sigmoid(hq)
    v = hi_ref[...].astype(F32)
    tri = _lower_tri(chunk)
    nchunk = r // chunk
    b = jnp.concatenate([_cumsum_rows(logf[c * chunk:(c + 1) * chunk], tri) for c in range(nchunk)],
                        axis=0)

    ones = jnp.ones((d, d), BF16)
    q3, k3, b3, v3 = [a.reshape(r // sub, sub, d) for a in (q, k, b, v)]
    row_in_sub = lax.broadcasted_iota(I32, (1, sub, 1), 1)
    o_diag = _dot((q * k).astype(BF16), ones) * v
    for off in range(1, sub):
        kr, br, vr = [pltpu.roll(a, off, 1) for a in (k3, b3, v3)]
        decay = jnp.exp(jnp.where(row_in_sub >= off, b3 - br, NEG_INF))
        prod = (q3 * kr * decay).reshape(r, d).astype(BF16)
        o_diag = o_diag + _dot(prod, ones) * vr.reshape(r, d)

    rowi = lax.broadcasted_iota(I32, (chunk, 1), 0)
    ri = lax.broadcasted_iota(I32, (chunk, chunk), 0)
    ci = lax.broadcasted_iota(I32, (chunk, chunk), 1)
    halves = []
    m = chunk // 2
    while m >= sub:
        sh = int(math.log2(2 * m))
        same_group = lax.shift_right_logical(ri, sh) == lax.shift_right_logical(ci, sh)
        halves.append((m, (rowi & (2 * m - 1)) >= m, same_group))
        m //= 2

    vb = v.astype(BF16)
    for c in range(nchunk):
        cs = slice(c * chunk, (c + 1) * chunk)
        qc, kc, bc, vc = q[cs], k[cs], b[cs], vb[cs]
        st = st_sc[...]
        blast = bc[chunk - 1:chunk, :]
        attn = jnp.zeros((chunk, chunk), F32)
        for m, upper, same_group in halves:
            anc = jnp.concatenate([jnp.broadcast_to(bc[g0 + m - 1:g0 + m, :], (2 * m, d))
                                   for g0 in range(0, chunk, 2 * m)], axis=0)
            q_m = (qc * jnp.exp(jnp.where(upper, bc - anc, NEG_INF))).astype(BF16)
            k_m = (kc * jnp.exp(jnp.where(upper, NEG_INF, anc - bc))).astype(BF16)
            attn = attn + jnp.where(same_group, _nt_dot(q_m, k_m), 0.0)
        oc_sc[cs, :] = (_nt_dot((qc * jnp.exp(bc)).astype(BF16), st.astype(BF16))
                        + _dot(attn.astype(BF16), vc))
        kd = (kc * jnp.exp(blast - bc)).astype(BF16)
        st_sc[...] = jnp.exp(blast) * st + _dot(vc.astype(F32).T.astype(BF16), kd)

    o = oc_sc[...] + o_diag
    o = _rms_norm(o, ng_ref[...])
    hg = hg_ref[...].astype(F32)
    o_ref[...] = (o * (hg * _sigmoid(hg))).astype(o_ref.dtype)


def _hgrn(hf, qig, llb, l1m, norm_g, s0t, batch, seq, rows):
    m, w = hf.shape
    nh = s0t.shape[0]
    d = w // nh
    rows = min(rows, seq)
    nblk = seq // rows
    assert seq % rows == 0 and rows % HG_CHUNK == 0 and m == batch * seq
    body = functools.partial(_hgrn_body, chunk=HG_CHUNK, sub=HG_SUB)
    blk = lambda g: pl.BlockSpec((rows, d), lambda b, h, c, g=g: (b * nblk + c, g * nh + h))
    vec = pl.BlockSpec((1, d), lambda b, h, c: (0, h))
    return pl.pallas_call(
        body,
        grid=(batch, nh, nblk),
        in_specs=[blk(0), blk(0), blk(1), blk(2), vec, vec, vec,
                  pl.BlockSpec((1, d, d), lambda b, h, c: (h, 0, 0))],
        out_specs=blk(0),
        out_shape=jax.ShapeDtypeStruct((m, w), BF16),
        scratch_shapes=[pltpu.VMEM((d, d), F32), pltpu.VMEM((rows, d), F32)],
        compiler_params=_params(("parallel", "parallel", "arbitrary")),
        name="hgrn",
    )(hf, qig, qig, qig, llb, l1m, norm_g, s0t)


def _ln1_body(h_ref, mix_ref, g_ref, b_ref, wr_ref, br_ref, o32_ref, o16_ref, lg_ref):
    y = _layer_norm(DN_ALPHA * h_ref[...] + mix_ref[...], g_ref[...], b_ref[...])
    o32_ref[...] = y
    o16_ref[...] = y.astype(BF16)
    lg_ref[...] = jnp.dot(y, wr_ref[...], precision=lax.Precision.HIGHEST,
                          preferred_element_type=F32) + br_ref[...]


def _ln1_router(h, mix, g, b, w_router, b_router, tm):
    m, d = h.shape
    ne = w_router.shape[1]
    row = pl.BlockSpec((tm, d), lambda i: (i, 0))
    vec = pl.BlockSpec((1, d), lambda i: (0, 0))
    return pl.pallas_call(
        _ln1_body,
        grid=(pl.cdiv(m, tm),),
        in_specs=[row, row, vec, vec,
                  pl.BlockSpec((d, ne), lambda i: (0, 0)), pl.BlockSpec((1, ne), lambda i: (0, 0))],
        out_specs=[row, row, pl.BlockSpec((tm, ne), lambda i: (i, 0))],
        out_shape=[jax.ShapeDtypeStruct((m, d), F32), jax.ShapeDtypeStruct((m, d), BF16),
                   jax.ShapeDtypeStruct((m, ne), F32)],
        compiler_params=_params(("parallel",)),
        name="ln1_router",
    )(h, mix, g.reshape(1, d), b.reshape(1, d), w_router, b_router.reshape(1, ne))


def _stream_expert_rows(start_ref, nblk_ref, x_hbm, o_hbm, x_buf, o_buf, x_sem, o_sem, setup, compute):
    e = pl.program_id(0)
    j = pl.program_id(1)
    rows, tn = o_buf.shape[1], o_buf.shape[2]
    base = start_ref[e]
    nb = nblk_ref[e]

    def x_copy(i, slot):
        r0 = pl.multiple_of(base + i * rows, rows)
        return pltpu.make_async_copy(x_hbm.at[pl.ds(r0, rows)], x_buf.at[slot], x_sem.at[slot])

    def o_copy(i, slot):
        r0 = pl.multiple_of(base + i * rows, rows)
        c0 = pl.multiple_of(j * tn, tn)
        return pltpu.make_async_copy(o_buf.at[slot], o_hbm.at[pl.ds(r0, rows), pl.ds(c0, tn)], o_sem.at[slot])

    @pl.when(nb > 0)
    def _():
        x_copy(0, 0).start(priority=1)

    setup()

    def row_block(i, carry):
        slot = i & 1
        x_copy(i, slot).wait()

        @pl.when(i + 1 < nb)
        def _():
            x_copy(i + 1, 1 - slot).start(priority=1)

        @pl.when(i >= 2)
        def _():
            o_copy(i - 2, slot).wait()

        o_buf[slot] = compute(x_buf[slot]).astype(o_buf.dtype)
        o_copy(i, slot).start(priority=1)
        return carry

    lax.fori_loop(0, nb, row_block, 0)

    @pl.when(nb >= 2)
    def _():
        o_copy(nb - 2, nb & 1).wait()

    @pl.when(nb >= 1)
    def _():
        o_copy(nb - 1, (nb - 1) & 1).wait()

    n_exp = pl.num_programs(0)

    @pl.when(e == n_exp - 1)
    def _():
        o_buf[0] = jnp.zeros(o_buf.shape[1:], o_buf.dtype)

        def zero_block(t, carry):
            r0 = pl.multiple_of(start_ref[n_exp] + t * rows, rows)
            c0 = pl.multiple_of(j * tn, tn)
            cp = pltpu.make_async_copy(o_buf.at[0], o_hbm.at[pl.ds(r0, rows), pl.ds(c0, tn)], o_sem.at[0])
            cp.start()
            cp.wait()
            return carry

        lax.fori_loop(0, nblk_ref[n_exp], zero_block, 0)


def _moe_up_body(start_ref, nblk_ref, x_hbm, wg_ref, bg_ref, wu_ref, bu_ref, o_hbm,
                 wg_sc, wu_sc, x_buf, o_buf, x_sem, o_sem):
    def cast_weights():
        wg_sc[...] = wg_ref[0].astype(BF16)
        wu_sc[...] = wu_ref[0].astype(BF16)

    def swiglu(x):
        a = jnp.minimum(_dot(x, wg_sc[...]) + bg_ref[0], SWIGLU_LIMIT)
        u = jnp.clip(_dot(x, wu_sc[...]) + bu_ref[0], -SWIGLU_LIMIT, SWIGLU_LIMIT)
        return (u + 1.0) * a * _sigmoid(SWIGLU_ALPHA * a)

    _stream_expert_rows(start_ref, nblk_ref, x_hbm, o_hbm, x_buf, o_buf, x_sem, o_sem, cast_weights, swiglu)


def _moe_down_body(start_ref, nblk_ref, a_hbm, wd_ref, bd_ref, o_hbm, wd_sc, a_buf, o_buf, a_sem, o_sem):
    def cast_weights():
        wd_sc[...] = wd_ref[0].astype(BF16)

    def down(a):
        return _dot(a, wd_sc[...]) + bd_ref[0]

    _stream_expert_rows(start_ref, nblk_ref, a_hbm, o_hbm, a_buf, o_buf, a_sem, o_sem, cast_weights, down)


def _moe_experts(xs, exp_start, exp_nblk, w_gate, b_gate, w_up, b_up, w_down, b_down, tn_up, tn_down):
    cap, d = xs.shape
    ne, _, f = w_gate.shape
    tn_up = min(tn_up, f)
    tn_down = min(tn_down, d)
    any_space = pl.BlockSpec(memory_space=pl.ANY)
    stream_scratch = lambda k, tn: [pltpu.VMEM((2, MOE_ROWS, k), BF16), pltpu.VMEM((2, MOE_ROWS, tn), BF16),
                                    pltpu.SemaphoreType.DMA((2,)), pltpu.SemaphoreType.DMA((2,))]
    w_tile = lambda k, tn: pl.BlockSpec((1, k, tn), lambda e, j, st, nb: (e, 0, j))
    act = pl.pallas_call(
        _moe_up_body,
        grid_spec=pltpu.PrefetchScalarGridSpec(
            num_scalar_prefetch=2,
            grid=(ne, f // tn_up),
            in_specs=[any_space, w_tile(d, tn_up), w_tile(1, tn_up), w_tile(d, tn_up), w_tile(1, tn_up)],
            out_specs=any_space,
            scratch_shapes=[pltpu.VMEM((d, tn_up), BF16), pltpu.VMEM((d, tn_up), BF16)]
            + stream_scratch(d, tn_up)),
        out_shape=jax.ShapeDtypeStruct((cap, f), BF16),
        compiler_params=_params(("arbitrary", "arbitrary")),
        name="moe_up",
    )(exp_start, exp_nblk, xs, w_gate, b_gate.reshape(ne, 1, f), w_up, b_up.reshape(ne, 1, f))
    return pl.pallas_call(
        _moe_down_body,
        grid_spec=pltpu.PrefetchScalarGridSpec(
            num_scalar_prefetch=2,
            grid=(ne, d // tn_down),
            in_specs=[any_space, w_tile(f, tn_down), w_tile(1, tn_down)],
            out_specs=any_space,
            scratch_shapes=[pltpu.VMEM((f, tn_down), BF16)] + stream_scratch(f, tn_down)),
        out_shape=jax.ShapeDtypeStruct((cap, d), BF16),
        compiler_params=_params(("arbitrary", "arbitrary")),
        name="moe_down",
    )(exp_start, exp_nblk, act, w_down, b_down.reshape(ne, 1, d))


def _moe_route(logits):
    n, ne = logits.shape
    nk = n * TOP_K
    top_val, top_idx = lax.top_k(logits, TOP_K)
    gate = jax.nn.softmax(top_val, axis=-1).reshape(nk)
    flat_e = top_idx.reshape(nk).astype(I32)
    onehot = (flat_e[:, None] == jnp.arange(ne, dtype=I32)[None, :]).astype(I32)
    csum = jnp.cumsum(onehot, axis=0)
    rank = jnp.sum(csum * onehot, axis=1) - 1
    counts = csum[-1]
    padded = (counts + MOE_ROWS - 1) // MOE_ROWS * MOE_ROWS
    pad_end = jnp.cumsum(padded)
    pos = (pad_end - padded)[flat_e] + rank
    nblk = -(-(nk + ne * (MOE_ROWS - 1)) // MOE_ROWS)
    cap = nblk * MOE_ROWS
    slot_pair = jnp.full((cap,), -1, I32).at[pos].set(jnp.arange(nk, dtype=I32), unique_indices=True)
    slot_pair = jnp.where(slot_pair >= 0, slot_pair, jnp.arange(cap, dtype=I32) % nk)
    slot_tok = slot_pair // TOP_K
    exp_start = jnp.concatenate([pad_end - padded, pad_end[-1:]]).astype(I32)
    exp_nblk = (jnp.concatenate([padded, cap - pad_end[-1:]]) // MOE_ROWS).astype(I32)
    return slot_tok, gate.reshape(n, TOP_K), pos.reshape(n, TOP_K), exp_start, exp_nblk


def _ln2_body(h_ref, y_ref, gate_ref, g_ref, b_ref, o_ref):
    gate = gate_ref[...]
    ffn = y_ref[0].astype(F32) * gate[:, 0:1]
    for c in range(1, y_ref.shape[0]):
        ffn = ffn + y_ref[c].astype(F32) * gate[:, c:c + 1]
    o_ref[...] = _layer_norm(DN_ALPHA * h_ref[...] + ffn, g_ref[...], b_ref[...])


def _ln2(h, y4, gate, g, b, tm):
    m, d = h.shape
    kk = y4.shape[0]
    row = pl.BlockSpec((tm, d), lambda i: (i, 0))
    vec = pl.BlockSpec((1, d), lambda i: (0, 0))
    return pl.pallas_call(
        _ln2_body,
        grid=(pl.cdiv(m, tm),),
        in_specs=[row, pl.BlockSpec((kk, tm, d), lambda i: (0, i, 0)),
                  pl.BlockSpec((tm, kk), lambda i: (i, 0)), vec, vec],
        out_specs=row,
        out_shape=jax.ShapeDtypeStruct((m, d), F32),
        compiler_params=_params(("parallel",)),
        name="ln2",
    )(h, y4, gate, g.reshape(1, d), b.reshape(1, d))


def _split_w_in(w_in):
    n_dsa = DSA_Q_RANK + DSA_KV_RANK + IDX_DIM + IDX_HEADS
    w_dsa = w_in[:, :n_dsa]
    w_hq, w_hf, w_hi, w_hg = [w_in[:, n_dsa + i * HG_WIDTH:n_dsa + (i + 1) * HG_WIDTH] for i in range(4)]
    w_qig = jnp.concatenate([w_hq, w_hi, w_hg], axis=1)
    return w_dsa.astype(BF16), w_hf.astype(BF16), w_qig.astype(BF16)


def _forward(x, meta_tokens, emb_ln_g, emb_ln_b, lb_logits, w_in, q_norm_g, kv_norm_g, w_uq, w_uk, w_uv,
             w_iq, idx_k_ln_g, idx_k_ln_b, hgrn_norm_g, w_o, ln1_g, ln1_b, w_router, b_router,
             w_gate, b_gate, w_up, b_up, w_down, b_down, ln2_g, ln2_b):
    batch, seq, d = x.shape
    n = batch * seq
    k_sel = min(IDX_TOPK_MAX, seq // 4)

    lb = jnp.cumsum(jax.nn.softmax(lb_logits.astype(F32), axis=0), axis=0)[0]
    llb = jnp.log(lb).reshape(1, HG_WIDTH)
    l1m = jnp.log1p(-lb).reshape(1, HG_WIDTH)
    w_dsa, w_hf, w_qig = _split_w_in(w_in)
    w_uqb = w_uq.astype(BF16)
    w_iqb = w_iq.astype(BF16)
    w_ukt3 = w_uk.transpose(1, 2, 0).astype(BF16)
    w_uv3 = w_uv.transpose(1, 0, 2).astype(BF16)

    def mixer_inputs(tokens, tm):
        h32, h16 = _ln_rows(tokens, emb_ln_g, emb_ln_b, 256)
        dsa_in = _matmul(h16, w_dsa, F32, 512, w_dsa.shape[1], "proj_dsa")
        hf = _matmul(h16, w_hf, F32, tm, 512, "proj_hf")
        qig = _matmul(h16, w_qig, BF16, tm, 512, "proj_qig")
        prep = _dsa_prep(dsa_in, q_norm_g, kv_norm_g, idx_k_ln_g, idx_k_ln_b, w_uqb, w_ukt3, w_iqb, 256)
        return h32, hf, qig, prep

    _, hf_m, qig_m, (_, _, ckv_m, _, _) = mixer_inputs(meta_tokens.astype(F32), N_META)
    s0t = _hgrn_state0(hf_m, qig_m[:, HG_WIDTH:2 * HG_WIDTH], llb, l1m, HG_HEADS)

    h0, hf, qig, (qa, iq, ckv, ikn, iwt) = mixer_inputs(x.reshape(n, d), 1024)
    a = _dsa_attention(qa, iq, iwt, ikn, ckv, ckv_m, w_uv3, batch, seq, k_sel, 128, 512)
    g = _hgrn(hf, qig, llb, l1m, hgrn_norm_g.reshape(1, HG_WIDTH), s0t, batch, seq, 1024)

    mix = _matmul_pair(a, g, w_o.astype(BF16), F32, 1024, 512, "proj_out")
    h1, h1b, logits = _ln1_router(h0, mix, ln1_g, ln1_b, w_router, b_router, 256)

    slot_tok, gate, pos, exp_start, exp_nblk = _moe_route(logits)
    xs = jnp.take(h1b, slot_tok, axis=0, mode="clip")
    y = _moe_experts(xs, exp_start, exp_nblk, w_gate, b_gate, w_up, b_up, w_down, b_down, 512, 2048)
    y4 = jnp.take(y, pos.T.reshape(TOP_K * n), axis=0, mode="clip").reshape(TOP_K, n, d)
    out = _ln2(h1, y4, gate, ln2_g, ln2_b, 128)
    return out.reshape(batch, seq, d)


def kernel(x, meta_tokens, emb_ln_g, emb_ln_b, lb_logits, w_in, q_norm_g, kv_norm_g, w_uq, w_uk, w_uv, w_iq, idx_k_ln_g, idx_k_ln_b, hgrn_norm_g, w_o, ln1_g, ln1_b, w_router, b_router, w_gate, b_gate, w_up, b_up, w_down, b_down, ln2_g, ln2_b):
    return _forward(x, meta_tokens, emb_ln_g, emb_ln_b, lb_logits, w_in[0], q_norm_g[0], kv_norm_g[0],
                    w_uq[0], w_uk[0], w_uv[0], w_iq[0], idx_k_ln_g[0], idx_k_ln_b[0], hgrn_norm_g[0],
                    w_o[0], ln1_g[0], ln1_b[0], w_router[0], b_router[0], w_gate[0], b_gate[0],
                    w_up[0], b_up[0], w_down[0], b_down[0], ln2_g[0], ln2_b[0])
```
